```python
import math, functools
import jax, jax.numpy as jnp
from jax import lax
import numpy as np

D_MODEL = 2048
BATCH = 4
SEQ = 8192
DEPTH = 1
DEC_BATCH = 8
DEC_SEQ = 32
PAST_LEN = 1024

CHUNK = 64
N_META = 16
SB_HEADS = 16
SB_HEAD_DIM = D_MODEL // SB_HEADS
SB_WIDTH = SB_HEADS * SB_HEAD_DIM
SB_BLOCK = 128
SB_SCALE = 1.0 / math.sqrt(SB_HEAD_DIM)
LRU_WIDTH = D_MODEL
LRU_HEADS = 16
LRU_HEAD_DIM = LRU_WIDTH // LRU_HEADS
CONV_WIDTH = 4
RG_C = 8.0
N_KEYS = 128
N_EXPERTS = N_KEYS * N_KEYS
PEER_HEADS = 8
PEER_KEY_DIM = 256
PEER_TOPK = 16
PEER_BLOCK = 128
W_IN_COLS = 3 * SB_WIDTH + 2 * LRU_WIDTH + 2 * D_MODEL
SPLITS = (SB_WIDTH, 2 * SB_WIDTH, 3 * SB_WIDTH, 3 * SB_WIDTH + LRU_WIDTH, 3 * SB_WIDTH + 2 * LRU_WIDTH)
RMS_EPS = 1e-6

kernel_name = 'stick_breaking_rglru_peer_stream_step'


def rmsnorm(x, g):
    xf = x.astype(jnp.float32)
    y = xf * lax.rsqrt(jnp.mean(xf * xf, axis=-1, keepdims=True) + RMS_EPS) * g.astype(jnp.float32)
    return y.astype(x.dtype)


def sb_block(q, k, v, q_pos, k_pos):
    z = jnp.einsum('bqhd,bkhd->bhqk', q, k).astype(jnp.float32) * SB_SCALE
    mask = k_pos[None, :] < q_pos[:, None]
    log_1m = jnp.where(mask, jax.nn.log_sigmoid(-z), 0.0)
    between = lax.cumsum(log_1m, axis=3, reverse=True) - log_1m
    w = jnp.where(mask, jnp.exp(jax.nn.log_sigmoid(z) + between), 0.0)
    return jnp.einsum('bhqk,bkhd->bqhd', w.astype(v.dtype), v)


def sb_prompt(q, k, v):
    b, t = q.shape[0], q.shape[1]
    n_blk = -(-t // SB_BLOCK)
    tp = n_blk * SB_BLOCK
    pad = ((0, 0), (0, tp - t), (0, 0), (0, 0))
    qp, kp, vp = jnp.pad(q, pad), jnp.pad(k, pad), jnp.pad(v, pad)
    q_blocks = qp.reshape(b, n_blk, SB_BLOCK, SB_HEADS, SB_HEAD_DIM).transpose(1, 0, 2, 3, 4)
    starts = jnp.arange(n_blk, dtype=jnp.int32) * SB_BLOCK
    k_pos = jnp.arange(tp, dtype=jnp.int32)

    def one(args):
        qb, s0 = args
        return sb_block(qb, kp, vp, s0 + jnp.arange(SB_BLOCK, dtype=jnp.int32), k_pos)

    o = lax.map(one, (q_blocks, starts))
    return o.transpose(1, 0, 2, 3, 4).reshape(b, tp, SB_HEADS, SB_HEAD_DIM)[:, :t]


def sb_sample(q, k, v, k_cache, v_cache):
    p, s = k_cache.shape[1], q.shape[1]
    k_all = jnp.concatenate([k_cache.astype(k.dtype), k], axis=1)
    v_all = jnp.concatenate([v_cache.astype(v.dtype), v], axis=1)
    return sb_block(q, k_all, v_all, p + jnp.arange(s, dtype=jnp.int32), jnp.arange(p + s, dtype=jnp.int32))


def causal_conv(x, buf, w, b):
    t = x.shape[1]
    xp = jnp.concatenate([buf.astype(x.dtype), x], axis=1)
    y = b
    for i in range(CONV_WIDTH):
        y = y + w[i] * xp[:, i:i + t]
    return y, xp[:, t:]


def block_diag(x, w, b):
    xh = x.reshape(x.shape[:-1] + (LRU_HEADS, LRU_HEAD_DIM))
    return jnp.einsum('bthi,hij->bthj', xh, w).reshape(x.shape) + b


def lin_combine(c1, c2):
    a1, b1 = c1
    a2, b2 = c2
    return (a1 * a2, a2 * b1 + b2)


def rglru(xc, h0, w_a, b_a, w_x, b_x, lam):
    xf = xc.astype(jnp.float32)
    r = jax.nn.sigmoid(block_diag(xf, w_a.astype(jnp.float32), b_a.astype(jnp.float32)))
    i = jax.nn.sigmoid(block_diag(xf, w_x.astype(jnp.float32), b_x.astype(jnp.float32)))
    log_a = -RG_C * r * jax.nn.softplus(-lam.astype(jnp.float32))
    a = jnp.exp(log_a)
    u = jnp.sqrt(-jnp.expm1(2.0 * log_a)) * (i * xf)
    u = u.at[:, 0].add(a[:, 0] * h0.astype(jnp.float32))
    _, h = lax.associative_scan(lin_combine, (a, u), axis=1)
    return h, h[:, -1]


def peer(x, w_query, sub_keys, expert_u, expert_v):
    shp = x.shape
    xf = x.reshape(-1, D_MODEL)
    n = xf.shape[0]
    n_blk = -(-n // PEER_BLOCK)
    xf = jnp.pad(xf, ((0, n_blk * PEER_BLOCK - n), (0, 0)))
    half = PEER_KEY_DIM // 2

    def one(xb):
        q = (xb @ w_query).reshape(PEER_BLOCK, PEER_HEADS, PEER_KEY_DIM)
        s1 = jnp.einsum('nhd,kd->nhk', q[..., :half], sub_keys[0]).astype(jnp.float32)
        s2 = jnp.einsum('nhd,kd->nhk', q[..., half:], sub_keys[1]).astype(jnp.float32)
        t1, i1 = lax.top_k(s1, PEER_TOPK)
        t2, i2 = lax.top_k(s2, PEER_TOPK)
        cand = (t1[..., :, None] + t2[..., None, :]).reshape(PEER_BLOCK, PEER_HEADS, PEER_TOPK * PEER_TOPK)
        score, c = lax.top_k(cand, PEER_TOPK)
        expert = (jnp.take_along_axis(i1, c // PEER_TOPK, axis=-1) * N_KEYS
                  + jnp.take_along_axis(i2, c % PEER_TOPK, axis=-1))
        gate = jax.nn.softmax(score, axis=-1)
        act = jax.nn.gelu(jnp.einsum('nhkd,nd->nhk', expert_u[expert], xb).astype(jnp.float32))
        return jnp.einsum('nhk,nhkd->nd', (gate * act).astype(xb.dtype), expert_v[expert])

    out = lax.map(one, xf.reshape(n_blk, PEER_BLOCK, D_MODEL))
    return out.reshape(-1, D_MODEL)[:n].reshape(shp)


def layer(x, attend, conv_buf, h0, norm_mix, norm_ffn, w_in, b_gate, conv_w, conv_b, w_rg_a, b_rg_a,
          w_rg_x, b_rg_x, lru_lambda, w_proj_attn, w_proj_lru, w_out, w_query, sub_keys, expert_u, expert_v):
    b, t, _ = x.shape
    xn = rmsnorm(x, norm_mix)
    q, k, v, xr, yg, gl = jnp.split(xn @ w_in, SPLITS, axis=-1)
    heads = (b, t, SB_HEADS, SB_HEAD_DIM)
    q, k, v = q.reshape(heads), k.reshape(heads), v.reshape(heads)
    y_sb = attend(q, k, v).reshape(b, t, SB_WIDTH) @ w_proj_attn
    xc, new_buf = causal_conv(xr, conv_buf, conv_w, conv_b)
    h, h_last = rglru(xc, h0, w_rg_a, b_rg_a, w_rg_x, b_rg_x, lru_lambda)
    y_lru = (jax.nn.gelu(yg) * h.astype(yg.dtype)) @ w_proj_lru
    g_sb, g_lru = jnp.split(jax.nn.sigmoid(gl + b_gate), 2, axis=-1)
    x = x + (g_sb * y_sb + g_lru * y_lru) @ w_out
    x = x + peer(rmsnorm(x, norm_ffn), w_query, sub_keys, expert_u, expert_v)
    return x, k, v, new_buf, h_last.astype(xn.dtype)


def setup_inputs(seed: int = 0) -> dict:
    key = jax.random.key(seed)
    ks = jax.random.split(key, 26)
    f = jnp.float32

    def nrm(k, shape, scale):
        return jax.random.normal(k, shape, f) * scale

    lam_u = jax.random.uniform(ks[17], (DEPTH, LRU_WIDTH), f, minval=0.9, maxval=0.999)
    return {
        'x_prompt': nrm(ks[0], (BATCH, SEQ, D_MODEL), 1.0),
        'x_sample': nrm(ks[1], (DEC_BATCH, DEC_SEQ, D_MODEL), 1.0),
        'cache_sb_k': nrm(ks[2], (DEPTH, DEC_BATCH, N_META + PAST_LEN, SB_HEADS, SB_HEAD_DIM), 1.0),
        'cache_sb_v': nrm(ks[3], (DEPTH, DEC_BATCH, N_META + PAST_LEN, SB_HEADS, SB_HEAD_DIM), 1.0),
        'state_conv': nrm(ks[4], (DEPTH, DEC_BATCH, CONV_WIDTH - 1, LRU_WIDTH), 1.0),
        'state_lru': nrm(ks[5], (DEPTH, DEC_BATCH, LRU_WIDTH), 0.5),
        'meta_tokens': nrm(ks[6], (N_META, D_MODEL), 1.0),
        'norm_mix': 1.0 + nrm(ks[7], (DEPTH, D_MODEL), 0.02),
        'norm_ffn': 1.0 + nrm(ks[8], (DEPTH, D_MODEL), 0.02),
        'w_in': nrm(ks[9], (DEPTH, D_MODEL, W_IN_COLS), D_MODEL ** -0.5),
        'b_gate': nrm(ks[10], (DEPTH, 2 * D_MODEL), 0.02),
        'conv_w': nrm(ks[11], (DEPTH, CONV_WIDTH, LRU_WIDTH), CONV_WIDTH ** -0.5),
        'conv_b': nrm(ks[12], (DEPTH, LRU_WIDTH), 0.02),
        'w_rg_a': nrm(ks[13], (DEPTH, LRU_HEADS, LRU_HEAD_DIM, LRU_HEAD_DIM), LRU_HEAD_DIM ** -0.5),
        'b_rg_a': nrm(ks[14], (DEPTH, LRU_WIDTH), 0.02),
        'w_rg_x': nrm(ks[15], (DEPTH, LRU_HEADS, LRU_HEAD_DIM, LRU_HEAD_DIM), LRU_HEAD_DIM ** -0.5),
        'b_rg_x': nrm(ks[16], (DEPTH, LRU_WIDTH), 0.02),
        'lru_lambda': jnp.log(lam_u) - jnp.log1p(-lam_u),
        'w_proj_attn': nrm(ks[18], (DEPTH, SB_WIDTH, D_MODEL), SB_WIDTH ** -0.5),
        'w_proj_lru': nrm(ks[19], (DEPTH, LRU_WIDTH, D_MODEL), LRU_WIDTH ** -0.5),
        'w_out': nrm(ks[20], (DEPTH, D_MODEL, D_MODEL), D_MODEL ** -0.5),
        'w_query': nrm(ks[21], (DEPTH, D_MODEL, PEER_HEADS * PEER_KEY_DIM), D_MODEL ** -0.5),
        'sub_keys': nrm(ks[22], (DEPTH, 2, N_KEYS, PEER_KEY_DIM // 2), (PEER_KEY_DIM // 2) ** -0.5),
        'expert_u': nrm(ks[23], (DEPTH, N_EXPERTS, D_MODEL), D_MODEL ** -0.5),
        'expert_v': nrm(ks[24], (DEPTH, N_EXPERTS, D_MODEL), 0.5),
        'norm_final': 1.0 + nrm(ks[25], (D_MODEL,), 0.02),
    }


def reference(x_prompt, x_sample, cache_sb_k, cache_sb_v, state_conv, state_lru, meta_tokens, norm_mix,
              norm_ffn, w_in, b_gate, conv_w, conv_b, w_rg_a, b_rg_a, w_rg_x, b_rg_x, lru_lambda,
              w_proj_attn, w_proj_lru, w_out, w_query, sub_keys, expert_u, expert_v, norm_final):
    b = x_prompt.shape[0]
    meta = jnp.broadcast_to(meta_tokens.astype(x_prompt.dtype)[None], (b, N_META, D_MODEL))
    xp = jnp.concatenate([meta, x_prompt], axis=1)
    xs = x_sample
    zero_buf = jnp.zeros((b, CONV_WIDTH - 1, LRU_WIDTH), x_prompt.dtype)
    zero_h = jnp.zeros((b, LRU_WIDTH), x_prompt.dtype)
    kp_l, vp_l, bp_l, hp_l, ks_l, vs_l, bs_l, hs_l = [], [], [], [], [], [], [], []
    for l in range(DEPTH):
        lw = [p[l] for p in (norm_mix, norm_ffn, w_in, b_gate, conv_w, conv_b, w_rg_a, b_rg_a, w_rg_x,
                             b_rg_x, lru_lambda, w_proj_attn, w_proj_lru, w_out, w_query, sub_keys,
                             expert_u, expert_v)]
        xp, kp, vp, bp, hp = layer(xp, sb_prompt, zero_buf, zero_h, *lw)
        attend_s = functools.partial(sb_sample, k_cache=cache_sb_k[l], v_cache=cache_sb_v[l])
        xs, k_s, v_s, b_s, h_s = layer(xs, attend_s, state_conv[l], state_lru[l], *lw)
        kp_l.append(kp); vp_l.append(vp); bp_l.append(bp); hp_l.append(hp)
        ks_l.append(k_s); vs_l.append(v_s); bs_l.append(b_s); hs_l.append(h_s)
    y_prompt = rmsnorm(xp[:, N_META:], norm_final)
    y_sample = rmsnorm(xs, norm_final)
    return (y_prompt, y_sample, jnp.stack(kp_l), jnp.stack(vp_l), jnp.stack(bp_l), jnp.stack(hp_l),
            jnp.stack(ks_l), jnp.stack(vs_l), jnp.stack(bs_l), jnp.stack(hs_l))
```

```python
import functools
import math

import jax
import jax.numpy as jnp
from jax import lax
from jax.experimental import pallas as pl
from jax.experimental.pallas import tpu as pltpu

F32 = jnp.float32
BF16 = jnp.bfloat16

LANES = 128
SUBLANES = 8
VMEM_LIMIT_BYTES = 56 * 1024 * 1024

RMS_EPS = 1e-6
RG_C = 8.0
CONV_WIDTH = 4
N_META = 16
PEER_TOPK = 16
PEER_HEADS = 8
N_KEYS = 128
SB_LOG_CUTOFF = -105.0
C_PITCH = 132

_NT = (((1,), (1,)), ((), ()))


def _params(*sem):
    return pltpu.CompilerParams(dimension_semantics=sem, vmem_limit_bytes=VMEM_LIMIT_BYTES)


def _row_block(n, target, align):
    best = None
    for t in range(align, min(n, target) + 1, align):
        if n % t == 0:
            best = t
    assert best is not None, (n, target, align)
    return best


def _softplus_neg_abs(z):
    return jnp.log1p(jnp.exp(-jnp.abs(z)))


def _split_bf16(x):
    hi = x.astype(BF16)
    lo = (x - hi.astype(F32)).astype(BF16)
    return hi, lo


def _rmsnorm_kernel(x_ref, g_ref, o_ref):
    x = x_ref[...]
    ms = jnp.mean(x * x, axis=-1, keepdims=True)
    o_ref[...] = (x * lax.rsqrt(ms + RMS_EPS) * g_ref[...]).astype(o_ref.dtype)


def _rmsnorm_cast(x, g):
    n, d = x.shape
    tm = _row_block(n, 512, 16)
    return pl.pallas_call(
        _rmsnorm_kernel,
        grid=(n // tm,),
        in_specs=[pl.BlockSpec((tm, d), lambda i: (i, 0)), pl.BlockSpec((1, d), lambda i: (0, 0))],
        out_specs=pl.BlockSpec((tm, d), lambda i: (i, 0)),
        out_shape=jax.ShapeDtypeStruct((n, d), BF16),
        compiler_params=_params("parallel"),
        name="rmsnorm_cast",
    )(x, g.reshape(1, d))


def _matmul_kernel(x_ref, w_ref, o_ref):
    o_ref[...] = jnp.dot(x_ref[...], w_ref[...], preferred_element_type=F32).astype(o_ref.dtype)


def _matmul_cols(x, w, col_block, ncols, out_dtype):
    n, k = x.shape
    tm = _row_block(n, 512, 16)
    return pl.pallas_call(
        _matmul_kernel,
        grid=(n // tm,),
        in_specs=[
            pl.BlockSpec((tm, k), lambda i: (i, 0)),
            pl.BlockSpec((k, ncols), lambda i: (0, col_block), pipeline_mode=pl.Buffered(1)),
        ],
        out_specs=pl.BlockSpec((tm, ncols), lambda i: (i, 0)),
        out_shape=jax.ShapeDtypeStruct((n, ncols), out_dtype),
        compiler_params=_params("parallel"),
        name="in_proj",
    )(x, w)


def _sb_kernel(q_ref, kn_ref, vn_ref, kc_ref, vc_ref, o_ref, kb_ref, vb_ref, *, n_new, n_cache, q_rows, scale):
    n_old_blocks = (n_cache - N_META) // LANES
    new0 = LANES * (1 + n_old_blocks)
    n_scratch = kb_ref.shape[0]

    zeros_head = jnp.zeros((LANES - N_META, LANES), BF16)
    kb_ref[0:LANES - N_META, :] = zeros_head
    vb_ref[0:LANES - N_META, :] = zeros_head
    kb_ref[LANES - N_META:LANES, :] = kc_ref[0:N_META, :].astype(BF16)
    vb_ref[LANES - N_META:LANES, :] = vc_ref[0:N_META, :].astype(BF16)

    def fill_old(c, carry):
        src = pl.multiple_of(N_META + c * LANES, SUBLANES)
        dst = pl.multiple_of(LANES + c * LANES, LANES)
        kb_ref[pl.ds(dst, LANES), :] = kc_ref[pl.ds(src, LANES), :].astype(BF16)
        vb_ref[pl.ds(dst, LANES), :] = vc_ref[pl.ds(src, LANES), :].astype(BF16)
        return carry

    if n_old_blocks:
        lax.fori_loop(0, n_old_blocks, fill_old, 0)

    fill_rows = min(n_new, LANES)

    def fill_new(c, carry):
        src = pl.multiple_of(c * fill_rows, fill_rows)
        dst = pl.multiple_of(new0 + c * fill_rows, fill_rows)
        kb_ref[pl.ds(dst, fill_rows), :] = kn_ref[pl.ds(src, fill_rows), :].astype(BF16)
        vb_ref[pl.ds(dst, fill_rows), :] = vn_ref[pl.ds(src, fill_rows), :].astype(BF16)
        return carry

    lax.fori_loop(0, n_new // fill_rows, fill_new, 0)
    if new0 + n_new < n_scratch:
        tail = jnp.zeros((n_scratch - new0 - n_new, LANES), BF16)
        kb_ref[new0 + n_new:n_scratch, :] = tail
        vb_ref[new0 + n_new:n_scratch, :] = tail

    row = lax.broadcasted_iota(jnp.int32, (q_rows, LANES), 0)
    col = lax.broadcasted_iota(jnp.int32, (q_rows, LANES), 1)
    kr = lax.broadcasted_iota(jnp.int32, (LANES, LANES), 0)
    kc = lax.broadcasted_iota(jnp.int32, (LANES, LANES), 1)
    later = jnp.concatenate([jnp.where(kr > kc, 1.0, 0.0), jnp.ones((LANES, LANES), F32)], axis=1).astype(BF16)

    def segment(qb, kblk, vblk, valid, carry, acc):
        z = lax.dot_general(qb, kblk, _NT, preferred_element_type=F32) * scale
        sp = _softplus_neg_abs(z)
        log_1m = jnp.where(valid, -(jnp.maximum(z, 0.0) + sp), 0.0)
        hi, lo = _split_bf16(log_1m)
        sums = (jnp.dot(hi, later, preferred_element_type=F32) + jnp.dot(lo, later, preferred_element_type=F32))
        between = sums[:, :LANES] + carry
        w = jnp.where(valid, jnp.exp(jnp.minimum(z, 0.0) - sp + between), 0.0)
        acc = acc + jnp.dot(w.astype(BF16), vblk, preferred_element_type=F32)
        return carry + sums[:, LANES:], acc

    def query_block(i, carry_unused):
        q0 = pl.multiple_of(i * q_rows, q_rows)
        qb = q_ref[pl.ds(q0, q_rows), :]
        d0 = pl.multiple_of(new0 + i * q_rows, q_rows)
        carry, acc = segment(qb, kb_ref[pl.ds(d0, LANES), :], vb_ref[pl.ds(d0, LANES), :], col < row,
                             jnp.zeros((q_rows, LANES), F32), jnp.zeros((q_rows, LANES), F32))

        def cond(c):
            j, carry, _ = c
            return jnp.logical_and(j >= 0, jnp.max(carry) > SB_LOG_CUTOFF)

        def body(c):
            j, carry, acc = c
            s0 = pl.multiple_of(j * LANES, LANES)
            valid = jnp.logical_or(j > 0, col >= LANES - N_META)
            carry, acc = segment(qb, kb_ref[pl.ds(s0, LANES), :], vb_ref[pl.ds(s0, LANES), :], valid, carry, acc)
            return j - 1, carry, acc

        j_first = (new0 + i * q_rows) // LANES - 1
        _, _, acc = lax.while_loop(cond, body, (j_first, carry, acc))
        o_ref[pl.ds(q0, q_rows), :] = acc.astype(o_ref.dtype)
        return carry_unused

    lax.fori_loop(0, n_new // q_rows, query_block, 0)


def _sb_attention(q, k_new, v_new, k_cache, v_cache, *, n_streams, n_new, n_cache, shared_cache, heads):
    assert (n_cache - N_META) % LANES == 0
    q_rows = min(n_new, LANES)
    assert n_new % q_rows == 0 and q_rows % 16 == 0
    new0 = LANES + (n_cache - N_META)
    n_scratch = new0 + (n_new // q_rows - 1) * q_rows + LANES
    cache_map = (lambda b, h: (0, h)) if shared_cache else (lambda b, h: (b, h))
    new_spec = pl.BlockSpec((n_new, LANES), lambda b, h: (b, h))
    cache_spec = pl.BlockSpec((n_cache, LANES), cache_map)
    kernel = functools.partial(_sb_kernel, n_new=n_new, n_cache=n_cache, q_rows=q_rows, scale=1.0 / math.sqrt(LANES))
    return pl.pallas_call(
        kernel,
        grid=(n_streams, heads),
        in_specs=[new_spec, new_spec, new_spec, cache_spec, cache_spec],
        out_specs=new_spec,
        out_shape=jax.ShapeDtypeStruct(q.shape, BF16),
        scratch_shapes=[pltpu.VMEM((n_scratch, LANES), BF16), pltpu.VMEM((n_scratch, LANES), BF16)],
        compiler_params=_params("parallel", "parallel"),
        name="sb_attention",
    )(q, k_new, v_new, k_cache, v_cache)


def _lru_kernel(xr_ref, yg_ref, buf_ref, h0_ref, cw_ref, cb_ref, wa_ref, ba_ref, wx_ref, bx_ref, lam_ref,
                m_ref, nbuf_ref, hl_ref, xp_ref, h_ref, *, n_rows, chunk):
    pad = SUBLANES
    n_hist = CONV_WIDTH - 1
    xp_ref[pad - n_hist:pad, :] = buf_ref[0]

    def copy(c, carry):
        r0 = pl.multiple_of(c * chunk, chunk)
        xp_ref[pl.ds(r0 + pad, chunk), :] = xr_ref[pl.ds(r0, chunk), :]
        return carry

    lax.fori_loop(0, n_rows // chunk, copy, 0)
    nbuf_ref[0] = xp_ref[n_rows + pad - n_hist:n_rows + pad, :]

    lam = lam_ref[...]
    sp_lam = jnp.maximum(-lam, 0.0) + _softplus_neg_abs(lam)
    cw = cw_ref[...]
    cb = cb_ref[...]
    wa = wa_ref[0]
    wx = wx_ref[0]
    ba = ba_ref[...]
    bx = bx_ref[...]
    groups = chunk // SUBLANES
    sub = lax.broadcasted_iota(jnp.int32, (groups, SUBLANES, LANES), 1)
    h_ref[...] = h0_ref[0]

    def step(c, carry):
        r0 = pl.multiple_of(c * chunk, chunk)
        xc = cb
        for tap in range(CONV_WIDTH):
            xc = xc + cw[tap:tap + 1, :] * xp_ref[pl.ds(r0 + pad - n_hist + tap, chunk), :]
        xcb = xc.astype(BF16)
        r = jax.nn.sigmoid(jnp.dot(xcb, wa, preferred_element_type=F32) + ba)
        gate_i = jax.nn.sigmoid(jnp.dot(xcb, wx, preferred_element_type=F32) + bx)
        log_a = -RG_C * r * sp_lam
        a = jnp.exp(log_a)
        t = jnp.tanh(log_a)
        u = jnp.sqrt(-2.0 * t / (1.0 - t)) * (gate_i * xc)
        a3 = a.reshape(groups, SUBLANES, LANES)
        u3 = u.reshape(groups, SUBLANES, LANES)
        shift = 1
        while shift < SUBLANES:
            keep = sub >= shift
            u3 = jnp.where(keep, a3 * pltpu.roll(u3, shift, axis=1) + u3, u3)
            a3 = jnp.where(keep, a3 * pltpu.roll(a3, shift, axis=1), a3)
            shift *= 2
        h = h_ref[...]
        rows = []
        for g in range(groups):
            hg = a3[g] * h + u3[g]
            h = hg[SUBLANES - 1:SUBLANES, :]
            rows.append(hg)
        h_ref[...] = h
        hs = jnp.concatenate(rows, axis=0)
        m_ref[pl.ds(r0, chunk), :] = (jax.nn.gelu(yg_ref[pl.ds(r0, chunk), :]) * hs).astype(m_ref.dtype)
        return carry

    lax.fori_loop(0, n_rows // chunk, step, 0)
    hl_ref[0] = h_ref[...]


def _conv_rglru(xr, yg, buf, h0, conv_w, conv_b, w_a, b_a, w_x, b_x, lam, *, n_streams, n_rows, shared_state):
    width = xr.shape[1]
    heads = width // LANES
    chunk = _row_block(n_rows, LANES, 16)
    state_map = (lambda b, c: (0, 0, c)) if shared_state else (lambda b, c: (b, 0, c))
    seq_spec = pl.BlockSpec((n_rows, LANES), lambda b, c: (b, c))
    vec_spec = pl.BlockSpec((1, LANES), lambda b, c: (0, c))
    gate_spec = pl.BlockSpec((1, LANES, LANES), lambda b, c: (c, 0, 0))
    kernel = functools.partial(_lru_kernel, n_rows=n_rows, chunk=chunk)
    return pl.pallas_call(
        kernel,
        grid=(n_streams, heads),
        in_specs=[seq_spec, seq_spec,
                  pl.BlockSpec((1, CONV_WIDTH - 1, LANES), state_map), pl.BlockSpec((1, 1, LANES), state_map),
                  pl.BlockSpec((CONV_WIDTH, LANES), lambda b, c: (0, c)), vec_spec,
                  gate_spec, vec_spec, gate_spec, vec_spec, vec_spec],
        out_specs=[seq_spec,
                   pl.BlockSpec((1, CONV_WIDTH - 1, LANES), lambda b, c: (b, 0, c)),
                   pl.BlockSpec((1, 1, LANES), lambda b, c: (b, 0, c))],
        out_shape=[jax.ShapeDtypeStruct(xr.shape, BF16),
                   jax.ShapeDtypeStruct((n_streams, CONV_WIDTH - 1, width), F32),
                   jax.ShapeDtypeStruct((n_streams, 1, width), F32)],
        scratch_shapes=[pltpu.VMEM((n_rows + SUBLANES, LANES), F32), pltpu.VMEM((1, LANES), F32)],
        compiler_params=_params("parallel", "parallel"),
        name="conv_rglru",
    )(xr, yg, buf, h0, conv_w, conv_b.reshape(1, width), w_a, b_a.reshape(1, width), w_x, b_x.reshape(1, width),
      lam.reshape(1, width))


def _mix_kernel(attn_ref, m_ref, gs_ref, gl_ref, bgs_ref, bgl_ref, x_ref, wpa_ref, wpl_ref, wo_ref, nf_ref,
                x1_ref, xn_ref):
    y_sb = jnp.dot(attn_ref[...], wpa_ref[...], preferred_element_type=F32)
    y_lru = jnp.dot(m_ref[...], wpl_ref[...], preferred_element_type=F32)
    g_sb = jax.nn.sigmoid(gs_ref[...] + bgs_ref[...])
    g_lru = jax.nn.sigmoid(gl_ref[...] + bgl_ref[...])
    mix = (g_sb * y_sb + g_lru * y_lru).astype(BF16)
    x1 = x_ref[...] + jnp.dot(mix, wo_ref[...], preferred_element_type=F32)
    x1_ref[...] = x1
    ms = jnp.mean(x1 * x1, axis=-1, keepdims=True)
    xn_ref[...] = (x1 * lax.rsqrt(ms + RMS_EPS) * nf_ref[...]).astype(xn_ref.dtype)


def _mix_project(attn, m, gs, gl, b_gate, x, w_pa, w_pl, w_o, norm_ffn):
    n, d = x.shape
    tm = _row_block(n, 256, 16)
    row = lambda i: (i, 0)
    fixed = lambda i: (0, 0)
    blk = pl.BlockSpec((tm, d), row)
    vec = pl.BlockSpec((1, d), fixed)
    wspec = pl.BlockSpec((d, d), fixed, pipeline_mode=pl.Buffered(1))
    return pl.pallas_call(
        _mix_kernel,
        grid=(n // tm,),
        in_specs=[blk, blk, blk, blk, vec, vec, blk, wspec, wspec, wspec, vec],
        out_specs=[blk, blk],
        out_shape=[jax.ShapeDtypeStruct((n, d), F32), jax.ShapeDtypeStruct((n, d), BF16)],
        compiler_params=_params("parallel"),
        name="mix_project",
    )(attn, m, gs, gl, b_gate[:d].reshape(1, d), b_gate[d:].reshape(1, d), x, w_pa, w_pl, w_o,
      norm_ffn.reshape(1, d))


def _top_rows(s, k):
    n = s.shape[0]
    rows = lax.broadcasted_iota(jnp.int32, s.shape, 0)
    vals, idxs = [], []
    for _ in range(k):
        m = jnp.max(s, axis=0, keepdims=True)
        idx = jnp.min(jnp.where(s == m, rows, n), axis=0, keepdims=True)
        s = jnp.where(rows == idx, -jnp.inf, s)
        vals.append(m)
        idxs.append(idx)
    return jnp.concatenate(vals, axis=0), jnp.concatenate(idxs, axis=0)


def _take_rows(table, idx):
    out = jnp.zeros(idx.shape, table.dtype)
    for r in range(table.shape[0]):
        out = jnp.where(idx == r, table[r:r + 1, :], out)
    return out


def _route_kernel(xn_ref, wq_ref, k1_ref, k2_ref, e1_ref, e2_ref, g_ref, q_ref, e1s_ref, e2s_ref, gs_ref):
    half = N_KEYS
    q_ref[...] = lax.dot_general(wq_ref[...], xn_ref[...], _NT, preferred_element_type=F32)
    k1 = k1_ref[...]
    k2 = k2_ref[...]

    def head(h, carry):
        base = pl.multiple_of(h * 2 * half, 2 * half)
        q1 = q_ref[pl.ds(base, half), :].astype(BF16)
        q2 = q_ref[pl.ds(base + half, half), :].astype(BF16)
        t1, i1 = _top_rows(jnp.dot(k1, q1, preferred_element_type=F32), PEER_TOPK)
        t2, i2 = _top_rows(jnp.dot(k2, q2, preferred_element_type=F32), PEER_TOPK)
        cand = jnp.concatenate([t1[a:a + 1, :] + t2 for a in range(PEER_TOPK)], axis=0)
        score, c = _top_rows(cand, PEER_TOPK)
        e1 = _take_rows(i1, lax.shift_right_logical(c, 4))
        e2 = _take_rows(i2, jnp.bitwise_and(c, PEER_TOPK - 1))
        ex = jnp.exp(score - jnp.max(score, axis=0, keepdims=True))
        gate = ex / jnp.sum(ex, axis=0, keepdims=True)
        out0 = pl.multiple_of(h * PEER_TOPK, PEER_TOPK)
        e1s_ref[pl.ds(out0, PEER_TOPK), :] = e1.astype(F32)
        e2s_ref[pl.ds(out0, PEER_TOPK), :] = e2.astype(F32)
        gs_ref[pl.ds(out0, PEER_TOPK), :] = gate
        return carry

    lax.fori_loop(0, PEER_HEADS, head, 0)
    e1_ref[...] = e1s_ref[...].T
    e2_ref[...] = e2s_ref[...].T
    g_ref[...] = gs_ref[...].T


def _peer_route(xn, wq_t, k1, k2):
    n, d = xn.shape
    tr = _row_block(n, 256, LANES)
    slots = PEER_HEADS * PEER_TOPK
    fixed = lambda i: (0, 0)
    out = pl.BlockSpec((tr, slots), lambda i: (i, 0))
    return pl.pallas_call(
        _route_kernel,
        grid=(n // tr,),
        in_specs=[pl.BlockSpec((tr, d), lambda i: (i, 0)),
                  pl.BlockSpec(wq_t.shape, fixed, pipeline_mode=pl.Buffered(1)),
                  pl.BlockSpec(k1.shape, fixed), pl.BlockSpec(k2.shape, fixed)],
        out_specs=[out, out, out],
        out_shape=[jax.ShapeDtypeStruct((n, slots), F32)] * 3,
        scratch_shapes=[pltpu.VMEM((wq_t.shape[0], tr), F32)] + [pltpu.VMEM((slots, tr), F32)] * 3,
        compiler_params=_params("parallel"),
        name="peer_route",
    )(xn, wq_t, k1, k2)


def _peer_kernel(xn_ref, e1_ref, e2_ref, g_ref, u_ref, v_ref, x1_ref, nf_ref, y_ref, c_ref, acc_ref, *, tn, pair):
    j = pl.program_id(1)

    @pl.when(j == 0)
    def _():
        acc_ref[...] = jnp.zeros_like(acc_ref)
        keys = lax.broadcasted_iota(jnp.int32, (N_KEYS, LANES), 0).astype(F32)

        def build(n, carry):
            g = g_ref[pl.ds(n, 1), :]
            first = keys == e1_ref[pl.ds(n, 1), :]
            g_hi = g.astype(BF16).astype(F32)
            lhs = jnp.concatenate([jnp.where(first, g_hi, 0.0), jnp.where(first, g - g_hi, 0.0)], axis=0).astype(BF16)
            second = jnp.where(keys == e2_ref[pl.ds(n, 1), :], 1.0, 0.0).astype(BF16)
            c2 = lax.dot_general(lhs, second, _NT, preferred_element_type=F32)
            c_ref[pl.ds(pl.multiple_of(n * C_PITCH, 4), N_KEYS), :] = c2[:N_KEYS] + c2[N_KEYS:]
            return carry

        lax.fori_loop(0, tn, build, 0)

    hidden = lax.dot_general(xn_ref[...], u_ref[...], _NT, preferred_element_type=F32)
    weighted = []
    for t in range(pair):
        c = c_ref[pl.ds(j * pair + t, tn, stride=C_PITCH), :]
        weighted.append((c * jax.nn.gelu(hidden[:, t * N_KEYS:(t + 1) * N_KEYS])).astype(BF16))
    acc_ref[...] += jnp.dot(jnp.concatenate(weighted, axis=1), v_ref[...], preferred_element_type=F32)

    @pl.when(j == pl.num_programs(1) - 1)
    def _():
        x2 = x1_ref[...] + acc_ref[...]
        ms = jnp.mean(x2 * x2, axis=-1, keepdims=True)
        y_ref[...] = x2 * lax.rsqrt(ms + RMS_EPS) * nf_ref[...]


def _peer_experts(xn, e1, e2, gate, u, v, x1, norm_final):
    n, d = xn.shape
    tn = _row_block(n, 256, 16)
    pair = 2
    rows = pair * N_KEYS
    blk = lambda i, j: (i, 0)
    kernel = functools.partial(_peer_kernel, tn=tn, pair=pair)
    return pl.pallas_call(
        kernel,
        grid=(n // tn, u.shape[0] // rows),
        in_specs=[pl.BlockSpec((tn, d), blk),
                  pl.BlockSpec((tn, LANES), blk), pl.BlockSpec((tn, LANES), blk), pl.BlockSpec((tn, LANES), blk),
                  pl.BlockSpec((rows, d), lambda i, j: (j, 0)), pl.BlockSpec((rows, d), lambda i, j: (j, 0)),
                  pl.BlockSpec((tn, d), blk), pl.BlockSpec((1, d), lambda i, j: (0, 0))],
        out_specs=pl.BlockSpec((tn, d), blk),
        out_shape=jax.ShapeDtypeStruct((n, d), F32),
        scratch_shapes=[pltpu.VMEM((tn * C_PITCH, LANES), F32), pltpu.VMEM((tn, d), F32)],
        compiler_params=_params("parallel", "arbitrary"),
        name="peer_experts",
    )(xn, e1, e2, gate, u, v, x1, norm_final.reshape(1, d))


def kernel(x_prompt, x_sample, cache_sb_k, cache_sb_v, state_conv, state_lru, meta_tokens, norm_mix, norm_ffn, w_in, b_gate, conv_w, conv_b, w_rg_a, b_rg_a, w_rg_x, b_rg_x, lru_lambda, w_proj_attn, w_proj_lru, w_out, w_query, sub_keys, expert_u, expert_v, norm_final):
    depth = w_in.shape[0]
    assert depth == 1, "one layer: the meta tokens' residual stream is never read after the mixer state"
    batch, seq, d = x_prompt.shape
    dec_batch, dec_seq, _ = x_sample.shape
    n_cache = cache_sb_k.shape[2]
    heads = cache_sb_k.shape[3]
    n_frames = batch * seq
    n_sample = dec_batch * dec_seq
    n_meta = meta_tokens.shape[0]
    assert n_meta == N_META and d == heads * LANES

    l = 0
    w_in_b = w_in[l].astype(BF16)
    w_a = w_rg_a[l].astype(BF16)
    w_x = w_rg_x[l].astype(BF16)
    w_pa = w_proj_attn[l].astype(BF16)
    w_pl = w_proj_lru[l].astype(BF16)
    w_o = w_out[l].astype(BF16)
    wq_t = w_query[l].T.astype(BF16)
    k1 = sub_keys[l, 0].astype(BF16)
    k2 = sub_keys[l, 1].astype(BF16)
    u_b = expert_u[l].astype(BF16)
    v_b = expert_v[l].astype(BF16)

    x_f = x_prompt.reshape(n_frames, d)
    x_s = jnp.concatenate([x_sample.reshape(n_sample, d), meta_tokens.astype(x_sample.dtype)], axis=0)

    def project(x):
        xn = _rmsnorm_cast(x, norm_mix[l])
        dts = (BF16, F32, F32, F32, F32, F32, F32)
        return [_matmul_cols(xn, w_in_b, g, d, dt) for g, dt in enumerate(dts)]

    q_f, k_f, v_f, xr_f, yg_f, gs_f, gl_f = project(x_f)
    q_s, k_s, v_s, xr_s, yg_s, gs_s, gl_s = project(x_s)
    k_meta, v_meta = k_s[n_sample:], v_s[n_sample:]

    lru_w = (conv_w[l], conv_b[l], w_a, b_rg_a[l], w_x, b_rg_x[l], lru_lambda[l])
    _, buf_meta, h_meta = _conv_rglru(
        xr_s[n_sample:], yg_s[n_sample:], jnp.zeros((1, CONV_WIDTH - 1, d), F32), jnp.zeros((1, 1, d), F32),
        *lru_w, n_streams=1, n_rows=n_meta, shared_state=True)
    m_f, buf_f, h_f = _conv_rglru(xr_f, yg_f, buf_meta, h_meta, *lru_w,
                                  n_streams=batch, n_rows=seq, shared_state=True)
    m_s, buf_s, h_s = _conv_rglru(xr_s[:n_sample], yg_s[:n_sample], state_conv[l], state_lru[l][:, None, :],
                                  *lru_w, n_streams=dec_batch, n_rows=dec_seq, shared_state=False)

    attn_f = _sb_attention(q_f, k_f, v_f, k_meta, v_meta, n_streams=batch, n_new=seq, n_cache=n_meta,
                           shared_cache=True, heads=heads)
    attn_s = _sb_attention(q_s[:n_sample], k_s[:n_sample], v_s[:n_sample],
                           cache_sb_k[l].reshape(dec_batch * n_cache, d), cache_sb_v[l].reshape(dec_batch * n_cache, d),
                           n_streams=dec_batch, n_new=dec_seq, n_cache=n_cache, shared_cache=False, heads=heads)

    def finish(attn, m, gs, gl, x):
        x1, xn2 = _mix_project(attn, m, gs, gl, b_gate[l], x, w_pa, w_pl, w_o, norm_ffn[l])
        e1, e2, gate = _peer_route(xn2, wq_t, k1, k2)
        return _peer_experts(xn2, e1, e2, gate, u_b, v_b, x1, norm_final)

    y_f = finish(attn_f, m_f, gs_f, gl_f, x_f)
    y_s = finish(attn_s, m_s[:n_sample], gs_s[:n_sample], gl_s[:n_sample], x_s[:n_sample])

    def with_meta(meta_rows, frames):
        meta_b = jnp.broadcast_to(meta_rows[None], (batch, n_meta, d))
        full = jnp.concatenate([meta_b, frames.reshape(batch, seq, d)], axis=1)
        return full.reshape(1, batch, n_meta + seq, heads, LANES)

    return (y_f.reshape(batch, seq, d), y_s.reshape(dec_batch, dec_seq, d),
            with_meta(k_meta, k_f), with_meta(v_meta, v_f),
            buf_f[None], h_f.reshape(1, batch, d),
            k_s[:n_sample].reshape(1, dec_batch, dec_seq, heads, LANES),
            v_s[:n_sample].reshape(1, dec_batch, dec_seq, heads, LANES),
            buf_s[None], h_s.reshape(1, dec_batch, d))
```

```python
import functools
import math

import jax
import jax.numpy as jnp
from jax import lax
from jax.experimental import pallas as pl
from jax.experimental.pallas import tpu as pltpu

F32 = jnp.float32
BF16 = jnp.bfloat16

LANES = 128
SUBLANES = 8
VMEM_LIMIT_BYTES = 56 * 1024 * 1024

RMS_EPS = 1e-6
RG_C = 8.0
CONV_WIDTH = 4
N_META = 16
PEER_TOPK = 16
PEER_HEADS = 8
N_KEYS = 128
SB_LOG_CUTOFF = -105.0
SB_MASKED = -1e30
SB_CHAINS = 8
C_PITCH = 132

_NT = (((1,), (1,)), ((), ()))


def _params(*sem):
    return pltpu.CompilerParams(dimension_semantics=sem, vmem_limit_bytes=VMEM_LIMIT_BYTES)


def _row_block(n, target, align):
    best = None
    for t in range(align, min(n, target) + 1, align):
        if n % t == 0:
            best = t
    assert best is not None, (n, target, align)
    return best


def _softplus_neg_abs(z):
    return jnp.log1p(jnp.exp(-jnp.abs(z)))


def _split_bf16(x):
    hi = x.astype(BF16)
    lo = (x - hi.astype(F32)).astype(BF16)
    return hi, lo


def _rmsnorm_kernel(x_ref, g_ref, o_ref):
    x = x_ref[...]
    ms = jnp.mean(x * x, axis=-1, keepdims=True)
    o_ref[...] = (x * lax.rsqrt(ms + RMS_EPS) * g_ref[...]).astype(o_ref.dtype)


def _rmsnorm_cast(x, g):
    n, d = x.shape
    tm = _row_block(n, 512, 16)
    return pl.pallas_call(
        _rmsnorm_kernel,
        grid=(n // tm,),
        in_specs=[pl.BlockSpec((tm, d), lambda i: (i, 0)), pl.BlockSpec((1, d), lambda i: (0, 0))],
        out_specs=pl.BlockSpec((tm, d), lambda i: (i, 0)),
        out_shape=jax.ShapeDtypeStruct((n, d), BF16),
        compiler_params=_params("parallel"),
        name="rmsnorm_cast",
    )(x, g.reshape(1, d))


def _matmul_kernel(x_ref, w_ref, o_ref):
    o_ref[...] = jnp.dot(x_ref[...], w_ref[...], preferred_element_type=F32).astype(o_ref.dtype)


def _matmul_cols(x, w, col_block, ncols, out_dtype):
    n, k = x.shape
    tm = _row_block(n, 512, 16)
    return pl.pallas_call(
        _matmul_kernel,
        grid=(n // tm,),
        in_specs=[
            pl.BlockSpec((tm, k), lambda i: (i, 0)),
            pl.BlockSpec((k, ncols), lambda i: (0, col_block), pipeline_mode=pl.Buffered(1)),
        ],
        out_specs=pl.BlockSpec((tm, ncols), lambda i: (i, 0)),
        out_shape=jax.ShapeDtypeStruct((n, ncols), out_dtype),
        compiler_params=_params("parallel"),
        name="in_proj",
    )(x, w)


def _sb_kernel(q_ref, kn_ref, vn_ref, kc_ref, vc_ref, o_ref, kb_ref, vb_ref, *, n_new, n_cache, q_rows, chains,
               scale):
    n_old_blocks = (n_cache - N_META) // LANES
    new0 = LANES * (1 + n_old_blocks)
    n_scratch = kb_ref.shape[0]

    zeros_head = jnp.zeros((LANES - N_META, LANES), BF16)
    kb_ref[0:LANES - N_META, :] = zeros_head
    vb_ref[0:LANES - N_META, :] = zeros_head
    kb_ref[LANES - N_META:LANES, :] = kc_ref[0:N_META, :].astype(BF16)
    vb_ref[LANES - N_META:LANES, :] = vc_ref[0:N_META, :].astype(BF16)

    def fill_old(c, carry):
        src = pl.multiple_of(N_META + c * LANES, SUBLANES)
        dst = pl.multiple_of(LANES + c * LANES, LANES)
        kb_ref[pl.ds(dst, LANES), :] = kc_ref[pl.ds(src, LANES), :].astype(BF16)
        vb_ref[pl.ds(dst, LANES), :] = vc_ref[pl.ds(src, LANES), :].astype(BF16)
        return carry

    if n_old_blocks:
        lax.fori_loop(0, n_old_blocks, fill_old, 0)

    fill_rows = min(n_new, LANES)

    def fill_new(c, carry):
        src = pl.multiple_of(c * fill_rows, fill_rows)
        dst = pl.multiple_of(new0 + c * fill_rows, fill_rows)
        kb_ref[pl.ds(dst, fill_rows), :] = kn_ref[pl.ds(src, fill_rows), :].astype(BF16)
        vb_ref[pl.ds(dst, fill_rows), :] = vn_ref[pl.ds(src, fill_rows), :].astype(BF16)
        return carry

    lax.fori_loop(0, n_new // fill_rows, fill_new, 0)
    if new0 + n_new < n_scratch:
        tail = jnp.zeros((n_scratch - new0 - n_new, LANES), BF16)
        kb_ref[new0 + n_new:n_scratch, :] = tail
        vb_ref[new0 + n_new:n_scratch, :] = tail

    row = lax.broadcasted_iota(jnp.int32, (q_rows, LANES), 0)
    col = lax.broadcasted_iota(jnp.int32, (q_rows, LANES), 1)
    kr = lax.broadcasted_iota(jnp.int32, (LANES, LANES), 0)
    kc = lax.broadcasted_iota(jnp.int32, (LANES, LANES), 1)
    later = jnp.concatenate([jnp.where(kr > kc, 1.0, 0.0), jnp.ones((LANES, LANES), F32)], axis=1).astype(BF16)

    def segments(qbs, blocks, biases, carries, accs):
        zs, sps, logs = [], [], []
        for qb, blk, bias in zip(qbs, blocks, biases):
            s0 = pl.multiple_of(blk * LANES, LANES)
            z = lax.dot_general(qb, kb_ref[pl.ds(s0, LANES), :], _NT, preferred_element_type=F32) * scale + bias
            sp = jnp.log(1.0 + jnp.exp(-jnp.abs(z)))
            zs.append(z)
            sps.append(sp)
            logs.append(-(jnp.maximum(z, 0.0) + sp))
        hi, lo = _split_bf16(jnp.concatenate(logs, axis=0))
        sums = jnp.dot(jnp.concatenate([hi, lo], axis=0), later, preferred_element_type=F32)
        sums = sums[:chains * q_rows] + sums[chains * q_rows:]
        new_carries, new_accs = [], []
        for c in range(chains):
            s = sums[c * q_rows:(c + 1) * q_rows]
            w = jnp.exp(jnp.minimum(zs[c], 0.0) - sps[c] + (s[:, :LANES] + carries[c]))
            v0 = pl.multiple_of(blocks[c] * LANES, LANES)
            new_accs.append(accs[c] + jnp.dot(w.astype(BF16), vb_ref[pl.ds(v0, LANES), :],
                                              preferred_element_type=F32))
            new_carries.append(carries[c] + s[:, LANES:])
        return tuple(new_carries), tuple(new_accs)

    causal_bias = jnp.where(col < row, 0.0, SB_MASKED)
    not_first_positions = jnp.where(col >= LANES - N_META, 0.0, 1.0)

    def query_group(g, carry_unused):
        q0s = [pl.multiple_of((g * chains + c) * q_rows, q_rows) for c in range(chains)]
        qbs = [q_ref[pl.ds(q0, q_rows), :] for q0 in q0s]
        diag = [(new0 + (g * chains + c) * q_rows) // LANES for c in range(chains)]
        zero = jnp.zeros((q_rows, LANES), F32)
        carries, accs = segments(qbs, diag, [causal_bias] * chains, (zero,) * chains, (zero,) * chains)

        def cond(state):
            dist, carries, _ = state
            go = jnp.bool_(False)
            for c in range(chains):
                go = jnp.logical_or(go, jnp.logical_and(diag[c] - dist >= 0, jnp.max(carries[c]) > SB_LOG_CUTOFF))
            return go

        def body(state):
            dist, carries, accs = state
            blocks = [jnp.maximum(diag[c] - dist, 0) for c in range(chains)]
            biases = [not_first_positions * jnp.where(diag[c] - dist == 0, SB_MASKED, 0.0)
                      + jnp.where(diag[c] - dist < 0, SB_MASKED, 0.0) for c in range(chains)]
            carries, accs = segments(qbs, blocks, biases, carries, accs)
            return dist + 1, carries, accs

        _, _, accs = lax.while_loop(cond, body, (jnp.int32(1), carries, accs))
        for c in range(chains):
            o_ref[pl.ds(q0s[c], q_rows), :] = accs[c].astype(o_ref.dtype)
        return carry_unused

    lax.fori_loop(0, n_new // (q_rows * chains), query_group, 0)


def _sb_attention(q, k_new, v_new, k_cache, v_cache, *, n_streams, n_new, n_cache, shared_cache, heads):
    assert (n_cache - N_META) % LANES == 0
    q_rows = min(n_new, LANES)
    assert n_new % q_rows == 0 and q_rows % 16 == 0
    new0 = LANES + (n_cache - N_META)
    n_scratch = new0 + (n_new // q_rows - 1) * q_rows + LANES
    cache_map = (lambda b, h: (0, h)) if shared_cache else (lambda b, h: (b, h))
    new_spec = pl.BlockSpec((n_new, LANES), lambda b, h: (b, h))
    cache_spec = pl.BlockSpec((n_cache, LANES), cache_map)
    n_qblocks = n_new // q_rows
    chains = SB_CHAINS if n_qblocks % SB_CHAINS == 0 else 1
    kernel = functools.partial(_sb_kernel, n_new=n_new, n_cache=n_cache, q_rows=q_rows, chains=chains,
                               scale=1.0 / math.sqrt(LANES))
    return pl.pallas_call(
        kernel,
        grid=(n_streams, heads),
        in_specs=[new_spec, new_spec, new_spec, cache_spec, cache_spec],
        out_specs=new_spec,
        out_shape=jax.ShapeDtypeStruct(q.shape, BF16),
        scratch_shapes=[pltpu.VMEM((n_scratch, LANES), BF16), pltpu.VMEM((n_scratch, LANES), BF16)],
        compiler_params=_params("parallel", "parallel"),
        name="sb_attention",
    )(q, k_new, v_new, k_cache, v_cache)


def _lru_kernel(xr_ref, yg_ref, buf_ref, h0_ref, cw_ref, cb_ref, wa_ref, ba_ref, wx_ref, bx_ref, lam_ref,
                m_ref, nbuf_ref, hl_ref, xp_ref, h_ref, *, n_rows, chunk):
    pad = SUBLANES
    n_hist = CONV_WIDTH - 1
    xp_ref[pad - n_hist:pad, :] = buf_ref[0]

    def copy(c, carry):
        r0 = pl.multiple_of(c * chunk, chunk)
        xp_ref[pl.ds(r0 + pad, chunk), :] = xr_ref[pl.ds(r0, chunk), :]
        return carry

    lax.fori_loop(0, n_rows // chunk, copy, 0)
    nbuf_ref[0] = xp_ref[n_rows + pad - n_hist:n_rows + pad, :]

    lam = lam_ref[...]
    sp_lam = jnp.maximum(-lam, 0.0) + _softplus_neg_abs(lam)
    cw = cw_ref[...]
    cb = cb_ref[...]
    wa = wa_ref[0]
    wx = wx_ref[0]
    ba = ba_ref[...]
    bx = bx_ref[...]
    groups = chunk // SUBLANES
    sub = lax.broadcasted_iota(jnp.int32, (groups, SUBLANES, LANES), 1)
    h_ref[...] = h0_ref[0]

    def step(c, carry):
        r0 = pl.multiple_of(c * chunk, chunk)
        xc = cb
        for tap in range(CONV_WIDTH):
            xc = xc + cw[tap:tap + 1, :] * xp_ref[pl.ds(r0 + pad - n_hist + tap, chunk), :]
        xcb = xc.astype(BF16)
        r = jax.nn.sigmoid(jnp.dot(xcb, wa, preferred_element_type=F32) + ba)
        gate_i = jax.nn.sigmoid(jnp.dot(xcb, wx, preferred_element_type=F32) + bx)
        log_a = -RG_C * r * sp_lam
        a = jnp.exp(log_a)
        t = jnp.tanh(log_a)
        u = jnp.sqrt(-2.0 * t / (1.0 - t)) * (gate_i * xc)
        a3 = a.reshape(groups, SUBLANES, LANES)
        u3 = u.reshape(groups, SUBLANES, LANES)
        shift = 1
        while shift < SUBLANES:
            keep = sub >= shift
            u3 = jnp.where(keep, a3 * pltpu.roll(u3, shift, axis=1) + u3, u3)
            a3 = jnp.where(keep, a3 * pltpu.roll(a3, shift, axis=1), a3)
            shift *= 2
        h = h_ref[...]
        rows = []
        for g in range(groups):
            hg = a3[g] * h + u3[g]
            h = hg[SUBLANES - 1:SUBLANES, :]
            rows.append(hg)
        h_ref[...] = h
        hs = jnp.concatenate(rows, axis=0)
        m_ref[pl.ds(r0, chunk), :] = (jax.nn.gelu(yg_ref[pl.ds(r0, chunk), :]) * hs).astype(m_ref.dtype)
        return carry

    lax.fori_loop(0, n_rows // chunk, step, 0)
    hl_ref[0] = h_ref[...]


def _conv_rglru(xr, yg, buf, h0, conv_w, conv_b, w_a, b_a, w_x, b_x, lam, *, n_streams, n_rows, shared_state):
    width = xr.shape[1]
    heads = width // LANES
    chunk = _row_block(n_rows, LANES, 16)
    state_map = (lambda b, c: (0, 0, c)) if shared_state else (lambda b, c: (b, 0, c))
    seq_spec = pl.BlockSpec((n_rows, LANES), lambda b, c: (b, c))
    vec_spec = pl.BlockSpec((1, LANES), lambda b, c: (0, c))
    gate_spec = pl.BlockSpec((1, LANES, LANES), lambda b, c: (c, 0, 0))
    kernel = functools.partial(_lru_kernel, n_rows=n_rows, chunk=chunk)
    return pl.pallas_call(
        kernel,
        grid=(n_streams, heads),
        in_specs=[seq_spec, seq_spec,
                  pl.BlockSpec((1, CONV_WIDTH - 1, LANES), state_map), pl.BlockSpec((1, 1, LANES), state_map),
                  pl.BlockSpec((CONV_WIDTH, LANES), lambda b, c: (0, c)), vec_spec,
                  gate_spec, vec_spec, gate_spec, vec_spec, vec_spec],
        out_specs=[seq_spec,
                   pl.BlockSpec((1, CONV_WIDTH - 1, LANES), lambda b, c: (b, 0, c)),
                   pl.BlockSpec((1, 1, LANES), lambda b, c: (b, 0, c))],
        out_shape=[jax.ShapeDtypeStruct(xr.shape, BF16),
                   jax.ShapeDtypeStruct((n_streams, CONV_WIDTH - 1, width), F32),
                   jax.ShapeDtypeStruct((n_streams, 1, width), F32)],
        scratch_shapes=[pltpu.VMEM((n_rows + SUBLANES, LANES), F32), pltpu.VMEM((1, LANES), F32)],
        compiler_params=_params("parallel", "parallel"),
        name="conv_rglru",
    )(xr, yg, buf, h0, conv_w, conv_b.reshape(1, width), w_a, b_a.reshape(1, width), w_x, b_x.reshape(1, width),
      lam.reshape(1, width))


def _mix_kernel(attn_ref, m_ref, gs_ref, gl_ref, bgs_ref, bgl_ref, x_ref, wpa_ref, wpl_ref, wo_ref, nf_ref,
                x1_ref, xn_ref):
    y_sb = jnp.dot(attn_ref[...], wpa_ref[...], preferred_element_type=F32)
    y_lru = jnp.dot(m_ref[...], wpl_ref[...], preferred_element_type=F32)
    g_sb = jax.nn.sigmoid(gs_ref[...] + bgs_ref[...])
    g_lru = jax.nn.sigmoid(gl_ref[...] + bgl_ref[...])
    mix = (g_sb * y_sb + g_lru * y_lru).astype(BF16)
    x1 = x_ref[...] + jnp.dot(mix, wo_ref[...], preferred_element_type=F32)
    x1_ref[...] = x1
    ms = jnp.mean(x1 * x1, axis=-1, keepdims=True)
    xn_ref[...] = (x1 * lax.rsqrt(ms + RMS_EPS) * nf_ref[...]).astype(xn_ref.dtype)


def _mix_project(attn, m, gs, gl, b_gate, x, w_pa, w_pl, w_o, norm_ffn):
    n, d = x.shape
    tm = _row_block(n, 256, 16)
    row = lambda i: (i, 0)
    fixed = lambda i: (0, 0)
    blk = pl.BlockSpec((tm, d), row)
    vec = pl.BlockSpec((1, d), fixed)
    wspec = pl.BlockSpec((d, d), fixed, pipeline_mode=pl.Buffered(1))
    return pl.pallas_call(
        _mix_kernel,
        grid=(n // tm,),
        in_specs=[blk, blk, blk, blk, vec, vec, blk, wspec, wspec, wspec, vec],
        out_specs=[blk, blk],
        out_shape=[jax.ShapeDtypeStruct((n, d), F32), jax.ShapeDtypeStruct((n, d), BF16)],
        compiler_params=_params("parallel"),
        name="mix_project",
    )(attn, m, gs, gl, b_gate[:d].reshape(1, d), b_gate[d:].reshape(1, d), x, w_pa, w_pl, w_o,
      norm_ffn.reshape(1, d))


def _top_rows(s, k):
    n = s.shape[0]
    rows = lax.broadcasted_iota(jnp.int32, s.shape, 0)
    vals, idxs = [], []
    for _ in range(k):
        m = jnp.max(s, axis=0, keepdims=True)
        idx = jnp.min(jnp.where(s == m, rows, n), axis=0, keepdims=True)
        s = jnp.where(rows == idx, -jnp.inf, s)
        vals.append(m)
        idxs.append(idx)
    return jnp.concatenate(vals, axis=0), jnp.concatenate(idxs, axis=0)


def _take_rows(table, idx):
    out = jnp.zeros(idx.shape, table.dtype)
    for r in range(table.shape[0]):
        out = jnp.where(idx == r, table[r:r + 1, :], out)
    return out


def _route_kernel(xn_ref, wq_ref, k1_ref, k2_ref, e1_ref, e2_ref, g_ref, q_ref, e1s_ref, e2s_ref, gs_ref):
    half = N_KEYS
    q_ref[...] = lax.dot_general(wq_ref[...], xn_ref[...], _NT, preferred_element_type=F32)
    k1 = k1_ref[...]
    k2 = k2_ref[...]

    def head(h, carry):
        base = pl.multiple_of(h * 2 * half, 2 * half)
        q1 = q_ref[pl.ds(base, half), :].astype(BF16)
        q2 = q_ref[pl.ds(base + half, half), :].astype(BF16)
        t1, i1 = _top_rows(jnp.dot(k1, q1, preferred_element_type=F32), PEER_TOPK)
        t2, i2 = _top_rows(jnp.dot(k2, q2, preferred_element_type=F32), PEER_TOPK)
        cand = jnp.concatenate(
            [t1[0:1, :] + t2]
            + [t1[a:a + 1, :] + t2[0:8, :] for a in range(1, 4)]
            + [t1[a:a + 1, :] + t2[0:4, :] for a in range(4, 8)]
            + [t1[8:16, :] + t2[0:1, :]], axis=0)
        score, r = _top_rows(cand, PEER_TOPK)
        rank1 = jnp.where(r < 16, 0, jnp.where(r < 40, 1 + lax.shift_right_logical(r - 16, 3),
                                               jnp.where(r < 56, 4 + lax.shift_right_logical(r - 40, 2), r - 48)))
        rank2 = jnp.where(r < 16, r, jnp.where(r < 40, jnp.bitwise_and(r - 16, 7),
                                               jnp.where(r < 56, jnp.bitwise_and(r - 40, 3), 0)))
        e1 = _take_rows(i1, rank1)
        e2 = _take_rows(i2, rank2)
        ex = jnp.exp(score - jnp.max(score, axis=0, keepdims=True))
        gate = ex / jnp.sum(ex, axis=0, keepdims=True)
        out0 = pl.multiple_of(h * PEER_TOPK, PEER_TOPK)
        e1s_ref[pl.ds(out0, PEER_TOPK), :] = e1.astype(F32)
        e2s_ref[pl.ds(out0, PEER_TOPK), :] = e2.astype(F32)
        gs_ref[pl.ds(out0, PEER_TOPK), :] = gate
        return carry

    lax.fori_loop(0, PEER_HEADS, head, 0)
    e1_ref[...] = e1s_ref[...].T
    e2_ref[...] = e2s_ref[...].T
    g_ref[...] = gs_ref[...].T


def _peer_route(xn, wq_t, k1, k2):
    n, d = xn.shape
    tr = _row_block(n, 256, LANES)
    slots = PEER_HEADS * PEER_TOPK
    fixed = lambda i: (0, 0)
    out = pl.BlockSpec((tr, slots), lambda i: (i, 0))
    return pl.pallas_call(
        _route_kernel,
        grid=(n // tr,),
        in_specs=[pl.BlockSpec((tr, d), lambda i: (i, 0)),
                  pl.BlockSpec(wq_t.shape, fixed, pipeline_mode=pl.Buffered(1)),
                  pl.BlockSpec(k1.shape, fixed), pl.BlockSpec(k2.shape, fixed)],
        out_specs=[out, out, out],
        out_shape=[jax.ShapeDtypeStruct((n, slots), F32)] * 3,
        scratch_shapes=[pltpu.VMEM((wq_t.shape[0], tr), F32)] + [pltpu.VMEM((slots, tr), F32)] * 3,
        compiler_params=_params("parallel"),
        name="peer_route",
    )(xn, wq_t, k1, k2)


PEER_CHUNK_KEYS = 2


def _gate_tile_bits(n, keys, e1_ref, e2_ref, g_ref):
    first = jnp.where(keys == e1_ref[pl.ds(n, 1), :], g_ref[pl.ds(n, 1), :], 0.0).astype(BF16)
    second = jnp.where(keys == e2_ref[pl.ds(n, 1), :], 1.0, 0.0).astype(BF16)
    c = lax.dot_general(first, second, _NT, preferred_element_type=F32)
    return lax.bitcast_convert_type(c.astype(BF16).astype(F32), jnp.uint32)


def _peer_kernel(xn_ref, e1_ref, e2_ref, g_ref, u0_ref, ua_ref, ub_ref, v_ref, x1_ref, nf_ref, y_ref,
                 c_ref, hid_a_ref, hid_b_ref, *, tn):
    j = pl.program_id(1)
    half = tn // 2
    ck = PEER_CHUNK_KEYS
    high = jnp.uint32(0xFFFF0000)

    @pl.when(j == 0)
    def _():
        y_ref[...] = jnp.zeros_like(y_ref)
        keys = lax.broadcasted_iota(jnp.int32, (N_KEYS, LANES), 0).astype(F32)

        def build(p, carry):
            bits = _gate_tile_bits(p, keys, e1_ref, e2_ref, g_ref)
            bits = bits | (_gate_tile_bits(p + half, keys, e1_ref, e2_ref, g_ref) >> 16)
            c_ref[pl.ds(pl.multiple_of(p * C_PITCH, 4), N_KEYS), :] = bits
            return carry

        lax.fori_loop(0, half, build, 0, unroll=8)
        hid_a_ref[...] = lax.dot_general(xn_ref[...], u0_ref[...], _NT, preferred_element_type=F32)

    def weighted(hid_ref, first_key):
        parts = []
        for t in range(ck):
            bits = c_ref[pl.ds(first_key + t, half, stride=C_PITCH), :]
            c = jnp.concatenate([lax.bitcast_convert_type(bits & high, F32),
                                 lax.bitcast_convert_type(bits << 16, F32)], axis=0)
            parts.append((c * jax.nn.gelu(hid_ref[:, t * N_KEYS:(t + 1) * N_KEYS])).astype(BF16))
        return parts

    xn = xn_ref[...]
    hid_b_ref[...] = lax.dot_general(xn, ua_ref[...], _NT, preferred_element_type=F32)
    parts = weighted(hid_a_ref, 2 * ck * j)
    hid_a_ref[...] = lax.dot_general(xn, ub_ref[...], _NT, preferred_element_type=F32)
    parts += weighted(hid_b_ref, 2 * ck * j + ck)
    y_ref[...] += jnp.dot(jnp.concatenate(parts, axis=1), v_ref[...], preferred_element_type=F32)

    @pl.when(j == pl.num_programs(1) - 1)
    def _():
        x2 = x1_ref[...] + y_ref[...]
        ms = jnp.mean(x2 * x2, axis=-1, keepdims=True)
        y_ref[...] = x2 * lax.rsqrt(ms + RMS_EPS) * nf_ref[...]


def _peer_experts(xn, e1, e2, gate, u, v, x1, norm_final):
    n, d = xn.shape
    tn = _row_block(n, 512, 32)
    rows = PEER_CHUNK_KEYS * N_KEYS
    n_chunks = u.shape[0] // rows
    assert n_chunks % 2 == 0
    blk = lambda i, j: (i, 0)
    tok = pl.BlockSpec((tn, LANES), blk)
    kernel = functools.partial(_peer_kernel, tn=tn)
    return pl.pallas_call(
        kernel,
        grid=(n // tn, n_chunks // 2),
        in_specs=[pl.BlockSpec((tn, d), blk), tok, tok, tok,
                  pl.BlockSpec((rows, d), lambda i, j: (0, 0)),
                  pl.BlockSpec((rows, d), lambda i, j: (2 * j + 1, 0)),
                  pl.BlockSpec((rows, d), lambda i, j: (jnp.minimum(2 * j + 2, n_chunks - 1), 0)),
                  pl.BlockSpec((2 * rows, d), lambda i, j: (j, 0)),
                  pl.BlockSpec((tn, d), blk, pipeline_mode=pl.Buffered(1)),
                  pl.BlockSpec((1, d), lambda i, j: (0, 0))],
        out_specs=pl.BlockSpec((tn, d), blk),
        out_shape=jax.ShapeDtypeStruct((n, d), F32),
        scratch_shapes=[pltpu.VMEM((tn // 2 * C_PITCH, LANES), jnp.uint32),
                        pltpu.VMEM((tn, rows), F32), pltpu.VMEM((tn, rows), F32)],
        compiler_params=_params("parallel", "arbitrary"),
        name="peer_experts",
    )(xn, e1, e2, gate, u, u, u, v, x1, norm_final.reshape(1, d))


def kernel(x_prompt, x_sample, cache_sb_k, cache_sb_v, state_conv, state_lru, meta_tokens, norm_mix, norm_ffn, w_in, b_gate, conv_w, conv_b, w_rg_a, b_rg_a, w_rg_x, b_rg_x, lru_lambda, w_proj_attn, w_proj_lru, w_out, w_query, sub_keys, expert_u, expert_v, norm_final):
    depth = w_in.shape[0]
    assert depth == 1, "one layer: the meta tokens' residual stream is never read after the mixer state"
    batch, seq, d = x_prompt.shape
    dec_batch, dec_seq, _ = x_sample.shape
    n_cache = cache_sb_k.shape[2]
    heads = cache_sb_k.shape[3]
    n_frames = batch * seq
    n_sample = dec_batch * dec_seq
    n_meta = meta_tokens.shape[0]
    assert n_meta == N_META and d == heads * LANES

    l = 0
    w_in_b = w_in[l].astype(BF16)
    w_a = w_rg_a[l].astype(BF16)
    w_x = w_rg_x[l].astype(BF16)
    w_pa = w_proj_attn[l].astype(BF16)
    w_pl = w_proj_lru[l].astype(BF16)
    w_o = w_out[l].astype(BF16)
    wq_t = w_query[l].T.astype(BF16)
    k1 = sub_keys[l, 0].astype(BF16)
    k2 = sub_keys[l, 1].astype(BF16)
    u_b = expert_u[l].astype(BF16)
    v_b = expert_v[l].astype(BF16)

    x_f = x_prompt.reshape(n_frames, d)
    x_s = jnp.concatenate([x_sample.reshape(n_sample, d), meta_tokens.astype(x_sample.dtype)], axis=0)

    def project(x):
        xn = _rmsnorm_cast(x, norm_mix[l])
        dts = (BF16, F32, F32, F32, F32, F32, F32)
        return [_matmul_cols(xn, w_in_b, g, d, dt) for g, dt in enumerate(dts)]

    q_f, k_f, v_f, xr_f, yg_f, gs_f, gl_f = project(x_f)
    q_s, k_s, v_s, xr_s, yg_s, gs_s, gl_s = project(x_s)
    k_meta, v_meta = k_s[n_sample:], v_s[n_sample:]

    lru_w = (conv_w[l], conv_b[l], w_a, b_rg_a[l], w_x, b_rg_x[l], lru_lambda[l])
    _, buf_meta, h_meta = _conv_rglru(
        xr_s[n_sample:], yg_s[n_sample:], jnp.zeros((1, CONV_WIDTH - 1, d), F32), jnp.zeros((1, 1, d), F32),
        *lru_w, n_streams=1, n_rows=n_meta, shared_state=True)
    m_f, buf_f, h_f = _conv_rglru(xr_f, yg_f, buf_meta, h_meta, *lru_w,
                                  n_streams=batch, n_rows=seq, shared_state=True)
    m_s, buf_s, h_s = _conv_rglru(xr_s[:n_sample], yg_s[:n_sample], state_conv[l], state_lru[l][:, None, :],
                                  *lru_w, n_streams=dec_batch, n_rows=dec_seq, shared_state=False)

    attn_f = _sb_attention(q_f, k_f, v_f, k_meta, v_meta, n_streams=batch, n_new=seq, n_cache=n_meta,
                           shared_cache=True, heads=heads)
    attn_s = _sb_attention(q_s[:n_sample], k_s[:n_sample], v_s[:n_sample],
                           cache_sb_k[l].reshape(dec_batch * n_cache, d), cache_sb_v[l].reshape(dec_batch * n_cache, d),
                           n_streams=dec_batch, n_new=dec_seq, n_cache=n_cache, shared_cache=False, heads=heads)

    def finish(attn, m, gs, gl, x):
        x1, xn2 = _mix_project(attn, m, gs, gl, b_gate[l], x, w_pa, w_pl, w_o, norm_ffn[l])
        e1, e2, gate = _peer_route(xn2, wq_t, k1, k2)
        return _peer_experts(xn2, e1, e2, gate, u_b, v_b, x1, norm_final)

    y_f = finish(attn_f, m_f, gs_f, gl_f, x_f)
    y_s = finish(attn_s, m_s[:n_sample], gs_s[:n_sample], gl_s[:n_sample], x_s[:n_sample])

    def with_meta(meta_rows, frames):
        meta_b = jnp.broadcast_to(meta_rows[None], (batch, n_meta, d))
        full = jnp.concatenate([meta_b, frames.reshape(batch, seq, d)], axis=1)
        return full.reshape(1, batch, n_meta + seq, heads, LANES)

    return (y_f.reshape(batch, seq, d), y_s.reshape(dec_batch, dec_seq, d),
            with_meta(k_meta, k_f), with_meta(v_meta, v_f),
            buf_f[None], h_f.reshape(1, batch, d),
            k_s[:n_sample].reshape(1, dec_batch, dec_seq, heads, LANES),
            v_s[:n_sample].reshape(1, dec_batch, dec_seq, heads, LANES),
            buf_s[None], h_s.reshape(1, dec_batch, d))
```

```python
import functools
import math

import jax
import jax.numpy as jnp
from jax import lax
from jax.experimental import pallas as pl
from jax.experimental.pallas import tpu as pltpu

F32 = jnp.float32
BF16 = jnp.bfloat16

LANES = 128
SUBLANES = 8
VMEM_LIMIT_BYTES = 56 * 1024 * 1024

RMS_EPS = 1e-6
RG_C = 8.0
CONV_WIDTH = 4
N_META = 16
PEER_TOPK = 16
PEER_HEADS = 8
N_KEYS = 128
SB_LOG_CUTOFF = -105.0
SB_MASKED = -1e30
SB_CHAINS = 8
C_PITCH = 132

_NT = (((1,), (1,)), ((), ()))


def _params(*sem):
    return pltpu.CompilerParams(dimension_semantics=sem, vmem_limit_bytes=VMEM_LIMIT_BYTES)


def _row_block(n, target, align):
    best = None
    for t in range(align, min(n, target) + 1, align):
        if n % t == 0:
            best = t
    assert best is not None, (n, target, align)
    return best


def _softplus_neg_abs(z):
    return jnp.log1p(jnp.exp(-jnp.abs(z)))


def _split_bf16(x):
    hi = x.astype(BF16)
    lo = (x - hi.astype(F32)).astype(BF16)
    return hi, lo


def _rmsnorm_kernel(x_ref, g_ref, o_ref):
    x = x_ref[...]
    ms = jnp.mean(x * x, axis=-1, keepdims=True)
    o_ref[...] = (x * lax.rsqrt(ms + RMS_EPS) * g_ref[...]).astype(o_ref.dtype)


def _rmsnorm_cast(x, g):
    n, d = x.shape
    tm = _row_block(n, 512, 16)
    return pl.pallas_call(
        _rmsnorm_kernel,
        grid=(n // tm,),
        in_specs=[pl.BlockSpec((tm, d), lambda i: (i, 0)), pl.BlockSpec((1, d), lambda i: (0, 0))],
        out_specs=pl.BlockSpec((tm, d), lambda i: (i, 0)),
        out_shape=jax.ShapeDtypeStruct((n, d), BF16),
        compiler_params=_params("parallel"),
        name="rmsnorm_cast",
    )(x, g.reshape(1, d))


def _matmul_kernel(x_ref, w_ref, o_ref):
    o_ref[...] = jnp.dot(x_ref[...], w_ref[...], preferred_element_type=F32).astype(o_ref.dtype)


def _matmul_cols(x, w, col_block, ncols, out_dtype):
    n, k = x.shape
    tm = _row_block(n, 512, 16)
    return pl.pallas_call(
        _matmul_kernel,
        grid=(n // tm,),
        in_specs=[
            pl.BlockSpec((tm, k), lambda i: (i, 0)),
            pl.BlockSpec((k, ncols), lambda i: (0, col_block), pipeline_mode=pl.Buffered(1)),
        ],
        out_specs=pl.BlockSpec((tm, ncols), lambda i: (i, 0)),
        out_shape=jax.ShapeDtypeStruct((n, ncols), out_dtype),
        compiler_params=_params("parallel"),
        name="in_proj",
    )(x, w)


def _sb_kernel(q_ref, kn_ref, vn_ref, kc_ref, vc_ref, o_ref, kb_ref, vb_ref, *, n_new, n_cache, q_rows, chains,
               scale):
    n_old_blocks = (n_cache - N_META) // LANES
    new0 = LANES * (1 + n_old_blocks)
    n_scratch = kb_ref.shape[0]

    zeros_head = jnp.zeros((LANES - N_META, LANES), BF16)
    kb_ref[0:LANES - N_META, :] = zeros_head
    vb_ref[0:LANES - N_META, :] = zeros_head
    kb_ref[LANES - N_META:LANES, :] = kc_ref[0:N_META, :].astype(BF16)
    vb_ref[LANES - N_META:LANES, :] = vc_ref[0:N_META, :].astype(BF16)

    def fill_old(c, carry):
        src = pl.multiple_of(N_META + c * LANES, SUBLANES)
        dst = pl.multiple_of(LANES + c * LANES, LANES)
        kb_ref[pl.ds(dst, LANES), :] = kc_ref[pl.ds(src, LANES), :].astype(BF16)
        vb_ref[pl.ds(dst, LANES), :] = vc_ref[pl.ds(src, LANES), :].astype(BF16)
        return carry

    if n_old_blocks:
        lax.fori_loop(0, n_old_blocks, fill_old, 0)

    fill_rows = min(n_new, LANES)

    def fill_new(c, carry):
        src = pl.multiple_of(c * fill_rows, fill_rows)
        dst = pl.multiple_of(new0 + c * fill_rows, fill_rows)
        kb_ref[pl.ds(dst, fill_rows), :] = kn_ref[pl.ds(src, fill_rows), :].astype(BF16)
        vb_ref[pl.ds(dst, fill_rows), :] = vn_ref[pl.ds(src, fill_rows), :].astype(BF16)
        return carry

    lax.fori_loop(0, n_new // fill_rows, fill_new, 0)
    if new0 + n_new < n_scratch:
        tail = jnp.zeros((n_scratch - new0 - n_new, LANES), BF16)
        kb_ref[new0 + n_new:n_scratch, :] = tail
        vb_ref[new0 + n_new:n_scratch, :] = tail

    row = lax.broadcasted_iota(jnp.int32, (q_rows, LANES), 0)
    col = lax.broadcasted_iota(jnp.int32, (q_rows, LANES), 1)
    kr = lax.broadcasted_iota(jnp.int32, (LANES, LANES), 0)
    kc = lax.broadcasted_iota(jnp.int32, (LANES, LANES), 1)
    later = jnp.concatenate([jnp.where(kr > kc, 1.0, 0.0), jnp.ones((LANES, LANES), F32)], axis=1).astype(BF16)

    def segments(qbs, blocks, biases, carries, accs):
        zs, sps, logs = [], [], []
        for qb, blk, bias in zip(qbs, blocks, biases):
            s0 = pl.multiple_of(blk * LANES, LANES)
            z = lax.dot_general(qb, kb_ref[pl.ds(s0, LANES), :], _NT, preferred_element_type=F32) * scale + bias
            sp = jnp.log(1.0 + jnp.exp(-jnp.abs(z)))
            zs.append(z)
            sps.append(sp)
            logs.append(-(jnp.maximum(z, 0.0) + sp))
        hi, lo = _split_bf16(jnp.concatenate(logs, axis=0))
        sums = jnp.dot(jnp.concatenate([hi, lo], axis=0), later, preferred_element_type=F32)
        sums = sums[:chains * q_rows] + sums[chains * q_rows:]
        new_carries, new_accs = [], []
        for c in range(chains):
            s = sums[c * q_rows:(c + 1) * q_rows]
            w = jnp.exp(jnp.minimum(zs[c], 0.0) - sps[c] + (s[:, :LANES] + carries[c]))
            v0 = pl.multiple_of(blocks[c] * LANES, LANES)
            new_accs.append(accs[c] + jnp.dot(w.astype(BF16), vb_ref[pl.ds(v0, LANES), :],
                                              preferred_element_type=F32))
            new_carries.append(carries[c] + s[:, LANES:])
        return tuple(new_carries), tuple(new_accs)

    causal_bias = jnp.where(col < row, 0.0, SB_MASKED)
    not_first_positions = jnp.where(col >= LANES - N_META, 0.0, 1.0)

    def query_group(g, carry_unused):
        q0s = [pl.multiple_of((g * chains + c) * q_rows, q_rows) for c in range(chains)]
        qbs = [q_ref[pl.ds(q0, q_rows), :] for q0 in q0s]
        diag = [(new0 + (g * chains + c) * q_rows) // LANES for c in range(chains)]
        zero = jnp.zeros((q_rows, LANES), F32)
        carries, accs = segments(qbs, diag, [causal_bias] * chains, (zero,) * chains, (zero,) * chains)

        def cond(state):
            dist, carries, _ = state
            go = jnp.bool_(False)
            for c in range(chains):
                go = jnp.logical_or(go, jnp.logical_and(diag[c] - dist >= 0, jnp.max(carries[c]) > SB_LOG_CUTOFF))
            return go

        def body(state):
            dist, carries, accs = state
            blocks = [jnp.maximum(diag[c] - dist, 0) for c in range(chains)]
            biases = [not_first_positions * jnp.where(diag[c] - dist == 0, SB_MASKED, 0.0)
                      + jnp.where(diag[c] - dist < 0, SB_MASKED, 0.0) for c in range(chains)]
            carries, accs = segments(qbs, blocks, biases, carries, accs)
            return dist + 1, carries, accs

        _, _, accs = lax.while_loop(cond, body, (jnp.int32(1), carries, accs))
        for c in range(chains):
            o_ref[pl.ds(q0s[c], q_rows), :] = accs[c].astype(o_ref.dtype)
        return carry_unused

    lax.fori_loop(0, n_new // (q_rows * chains), query_group, 0)


def _sb_attention(q, k_new, v_new, k_cache, v_cache, *, n_streams, n_new, n_cache, shared_cache, heads):
    assert (n_cache - N_META) % LANES == 0
    q_rows = min(n_new, LANES)
    assert n_new % q_rows == 0 and q_rows % 16 == 0
    new0 = LANES + (n_cache - N_META)
    n_scratch = new0 + (n_new // q_rows - 1) * q_rows + LANES
    cache_map = (lambda b, h: (0, h)) if shared_cache else (lambda b, h: (b, h))
    new_spec = pl.BlockSpec((n_new, LANES), lambda b, h: (b, h))
    cache_spec = pl.BlockSpec((n_cache, LANES), cache_map)
    n_qblocks = n_new // q_rows
    chains = SB_CHAINS if n_qblocks % SB_CHAINS == 0 else 1
    kernel = functools.partial(_sb_kernel, n_new=n_new, n_cache=n_cache, q_rows=q_rows, chains=chains,
                               scale=1.0 / math.sqrt(LANES))
    return pl.pallas_call(
        kernel,
        grid=(n_streams, heads),
        in_specs=[new_spec, new_spec, new_spec, cache_spec, cache_spec],
        out_specs=new_spec,
        out_shape=jax.ShapeDtypeStruct(q.shape, BF16),
        scratch_shapes=[pltpu.VMEM((n_scratch, LANES), BF16), pltpu.VMEM((n_scratch, LANES), BF16)],
        compiler_params=_params("parallel", "parallel"),
        name="sb_attention",
    )(q, k_new, v_new, k_cache, v_cache)


def _lru_kernel(xr_ref, yg_ref, buf_ref, h0_ref, cw_ref, cb_ref, wa_ref, ba_ref, wx_ref, bx_ref, lam_ref,
                m_ref, nbuf_ref, hl_ref, xp_ref, h_ref, *, n_rows, chunk):
    pad = SUBLANES
    n_hist = CONV_WIDTH - 1
    xp_ref[pad - n_hist:pad, :] = buf_ref[0]

    def copy(c, carry):
        r0 = pl.multiple_of(c * chunk, chunk)
        xp_ref[pl.ds(r0 + pad, chunk), :] = xr_ref[pl.ds(r0, chunk), :]
        return carry

    lax.fori_loop(0, n_rows // chunk, copy, 0)
    nbuf_ref[0] = xp_ref[n_rows + pad - n_hist:n_rows + pad, :]

    lam = lam_ref[...]
    sp_lam = jnp.maximum(-lam, 0.0) + _softplus_neg_abs(lam)
    cw = cw_ref[...]
    cb = cb_ref[...]
    wa = wa_ref[0]
    wx = wx_ref[0]
    ba = ba_ref[...]
    bx = bx_ref[...]
    groups = chunk // SUBLANES
    sub = lax.broadcasted_iota(jnp.int32, (groups, SUBLANES, LANES), 1)
    h_ref[...] = h0_ref[0]

    def step(c, carry):
        r0 = pl.multiple_of(c * chunk, chunk)
        xc = cb
        for tap in range(CONV_WIDTH):
            xc = xc + cw[tap:tap + 1, :] * xp_ref[pl.ds(r0 + pad - n_hist + tap, chunk), :]
        xcb = xc.astype(BF16)
        r = jax.nn.sigmoid(jnp.dot(xcb, wa, preferred_element_type=F32) + ba)
        gate_i = jax.nn.sigmoid(jnp.dot(xcb, wx, preferred_element_type=F32) + bx)
        log_a = -RG_C * r * sp_lam
        a = jnp.exp(log_a)
        t = jnp.tanh(log_a)
        u = jnp.sqrt(-2.0 * t / (1.0 - t)) * (gate_i * xc)
        a3 = a.reshape(groups, SUBLANES, LANES)
        u3 = u.reshape(groups, SUBLANES, LANES)
        shift = 1
        while shift < SUBLANES:
            keep = sub >= shift
            u3 = jnp.where(keep, a3 * pltpu.roll(u3, shift, axis=1) + u3, u3)
            a3 = jnp.where(keep, a3 * pltpu.roll(a3, shift, axis=1), a3)
            shift *= 2
        h = h_ref[...]
        rows = []
        for g in range(groups):
            hg = a3[g] * h + u3[g]
            h = hg[SUBLANES - 1:SUBLANES, :]
            rows.append(hg)
        h_ref[...] = h
        hs = jnp.concatenate(rows, axis=0)
        m_ref[pl.ds(r0, chunk), :] = (jax.nn.gelu(yg_ref[pl.ds(r0, chunk), :]) * hs).astype(m_ref.dtype)
        return carry

    n_chunks = n_rows // chunk
    lax.fori_loop(0, n_chunks, step, 0, unroll=2 if n_chunks % 2 == 0 else 1)
    hl_ref[0] = h_ref[...]


def _conv_rglru(xr, yg, buf, h0, conv_w, conv_b, w_a, b_a, w_x, b_x, lam, *, n_streams, n_rows, shared_state):
    width = xr.shape[1]
    heads = width // LANES
    chunk = _row_block(n_rows, LANES, 16)
    state_map = (lambda b, c: (0, 0, c)) if shared_state else (lambda b, c: (b, 0, c))
    seq_spec = pl.BlockSpec((n_rows, LANES), lambda b, c: (b, c))
    vec_spec = pl.BlockSpec((1, LANES), lambda b, c: (0, c))
    gate_spec = pl.BlockSpec((1, LANES, LANES), lambda b, c: (c, 0, 0))
    kernel = functools.partial(_lru_kernel, n_rows=n_rows, chunk=chunk)
    return pl.pallas_call(
        kernel,
        grid=(n_streams, heads),
        in_specs=[seq_spec, seq_spec,
                  pl.BlockSpec((1, CONV_WIDTH - 1, LANES), state_map), pl.BlockSpec((1, 1, LANES), state_map),
                  pl.BlockSpec((CONV_WIDTH, LANES), lambda b, c: (0, c)), vec_spec,
                  gate_spec, vec_spec, gate_spec, vec_spec, vec_spec],
        out_specs=[seq_spec,
                   pl.BlockSpec((1, CONV_WIDTH - 1, LANES), lambda b, c: (b, 0, c)),
                   pl.BlockSpec((1, 1, LANES), lambda b, c: (b, 0, c))],
        out_shape=[jax.ShapeDtypeStruct(xr.shape, BF16),
                   jax.ShapeDtypeStruct((n_streams, CONV_WIDTH - 1, width), F32),
                   jax.ShapeDtypeStruct((n_streams, 1, width), F32)],
        scratch_shapes=[pltpu.VMEM((n_rows + SUBLANES, LANES), F32), pltpu.VMEM((1, LANES), F32)],
        compiler_params=_params("parallel", "parallel"),
        name="conv_rglru",
    )(xr, yg, buf, h0, conv_w, conv_b.reshape(1, width), w_a, b_a.reshape(1, width), w_x, b_x.reshape(1, width),
      lam.reshape(1, width))


def _mix_kernel(attn_ref, m_ref, gs_ref, gl_ref, bgs_ref, bgl_ref, x_ref, wpa_ref, wpl_ref, wo_ref, nf_ref,
                x1_ref, xn_ref):
    y_sb = jnp.dot(attn_ref[...], wpa_ref[...], preferred_element_type=F32)
    y_lru = jnp.dot(m_ref[...], wpl_ref[...], preferred_element_type=F32)
    g_sb = jax.nn.sigmoid(gs_ref[...] + bgs_ref[...])
    g_lru = jax.nn.sigmoid(gl_ref[...] + bgl_ref[...])
    mix = (g_sb * y_sb + g_lru * y_lru).astype(BF16)
    x1 = x_ref[...] + jnp.dot(mix, wo_ref[...], preferred_element_type=F32)
    x1_ref[...] = x1
    ms = jnp.mean(x1 * x1, axis=-1, keepdims=True)
    xn_ref[...] = (x1 * lax.rsqrt(ms + RMS_EPS) * nf_ref[...]).astype(xn_ref.dtype)


def _mix_project(attn, m, gs, gl, b_gate, x, w_pa, w_pl, w_o, norm_ffn):
    n, d = x.shape
    tm = _row_block(n, 256, 16)
    row = lambda i: (i, 0)
    fixed = lambda i: (0, 0)
    blk = pl.BlockSpec((tm, d), row)
    vec = pl.BlockSpec((1, d), fixed)
    wspec = pl.BlockSpec((d, d), fixed, pipeline_mode=pl.Buffered(1))
    return pl.pallas_call(
        _mix_kernel,
        grid=(n // tm,),
        in_specs=[blk, blk, blk, blk, vec, vec, blk, wspec, wspec, wspec, vec],
        out_specs=[blk, blk],
        out_shape=[jax.ShapeDtypeStruct((n, d), F32), jax.ShapeDtypeStruct((n, d), BF16)],
        compiler_params=_params("parallel"),
        name="mix_project",
    )(attn, m, gs, gl, b_gate[:d].reshape(1, d), b_gate[d:].reshape(1, d), x, w_pa, w_pl, w_o,
      norm_ffn.reshape(1, d))


def _top_rows(s, k):
    n = s.shape[0]
    rows = lax.broadcasted_iota(jnp.int32, s.shape, 0)
    vals, idxs = [], []
    for _ in range(k):
        m = jnp.max(s, axis=0, keepdims=True)
        idx = jnp.min(jnp.where(s == m, rows, n), axis=0, keepdims=True)
        s = jnp.where(rows == idx, -jnp.inf, s)
        vals.append(m)
        idxs.append(idx)
    return jnp.concatenate(vals, axis=0), jnp.concatenate(idxs, axis=0)


def _take_rows(table, idx):
    out = jnp.zeros(idx.shape, table.dtype)
    for r in range(table.shape[0]):
        out = jnp.where(idx == r, table[r:r + 1, :], out)
    return out


def _route_unit(q1, q2, k1, k2):
    return _route_combine(_route_side(q1, k1), _route_side(q2, k2))


def _route_side(q, k):
    return _top_rows(lax.dot_general(k, q, _NT, preferred_element_type=F32), PEER_TOPK)


def _route_combine(side1, side2):
    t1, i1 = side1
    t2, i2 = side2
    cand = jnp.concatenate(
        [t1[0:1, :] + t2]
        + [t1[a:a + 1, :] + t2[0:8, :] for a in range(1, 4)]
        + [t1[a:a + 1, :] + t2[0:4, :] for a in range(4, 8)]
        + [t1[8:16, :] + t2[0:1, :]], axis=0)
    score, r = _top_rows(cand, PEER_TOPK)
    rank1 = jnp.where(r < 16, 0, jnp.where(r < 40, 1 + lax.shift_right_logical(r - 16, 3),
                                           jnp.where(r < 56, 4 + lax.shift_right_logical(r - 40, 2), r - 48)))
    rank2 = jnp.where(r < 16, r, jnp.where(r < 40, jnp.bitwise_and(r - 16, 7),
                                           jnp.where(r < 56, jnp.bitwise_and(r - 40, 3), 0)))
    e1 = _take_rows(i1, rank1)
    e2 = _take_rows(i2, rank2)
    ex = jnp.exp(score - jnp.max(score, axis=0, keepdims=True))
    gate = ex / jnp.sum(ex, axis=0, keepdims=True)
    return e1.astype(F32), e2.astype(F32), gate


def _query_kernel(x_ref, w_ref, o_ref):
    q = jnp.dot(x_ref[...], w_ref[...], preferred_element_type=F32).astype(o_ref.dtype)
    for part in range(o_ref.shape[0]):
        o_ref[part] = q[:, part * LANES:(part + 1) * LANES]


def _query_proj(xn, wq):
    n, d = xn.shape
    parts = wq.shape[1] // LANES
    tm = _row_block(n, 512, 16)
    return pl.pallas_call(
        _query_kernel,
        grid=(n // tm,),
        in_specs=[pl.BlockSpec((tm, d), lambda i: (i, 0)),
                  pl.BlockSpec(wq.shape, lambda i: (0, 0), pipeline_mode=pl.Buffered(1))],
        out_specs=pl.BlockSpec((parts, tm, LANES), lambda i: (0, i, 0)),
        out_shape=jax.ShapeDtypeStruct((parts, n, LANES), BF16),
        compiler_params=_params("parallel"),
        name="query_proj",
    )(xn, wq)


def _route_kernel(q_ref, k1_ref, k2_ref, e1_ref, e2_ref, g_ref, e1s_ref, e2s_ref, gs_ref):
    k1 = k1_ref[...]
    k2 = k2_ref[...]

    def head(h, carry):
        e1, e2, gate = _route_unit(q_ref[2 * h], q_ref[2 * h + 1], k1, k2)
        out0 = pl.multiple_of(h * PEER_TOPK, PEER_TOPK)
        e1s_ref[pl.ds(out0, PEER_TOPK), :] = e1
        e2s_ref[pl.ds(out0, PEER_TOPK), :] = e2
        gs_ref[pl.ds(out0, PEER_TOPK), :] = gate
        return carry

    lax.fori_loop(0, PEER_HEADS, head, 0)
    e1_ref[...] = e1s_ref[...].T
    e2_ref[...] = e2s_ref[...].T
    g_ref[...] = gs_ref[...].T


def _peer_route_first(q3, k1, k2, n_first):
    parts = q3.shape[0]
    slots = PEER_HEADS * PEER_TOPK
    fixed = lambda i: (0, 0)
    out = pl.BlockSpec((LANES, slots), lambda i: (i, 0))
    return pl.pallas_call(
        _route_kernel,
        grid=(n_first // LANES,),
        in_specs=[pl.BlockSpec((parts, LANES, LANES), lambda i: (0, i, 0)),
                  pl.BlockSpec(k1.shape, fixed), pl.BlockSpec(k2.shape, fixed)],
        out_specs=[out, out, out],
        out_shape=[jax.ShapeDtypeStruct((n_first, slots), F32)] * 3,
        scratch_shapes=[pltpu.VMEM((slots, LANES), F32)] * 3,
        compiler_params=_params("parallel"),
        name="peer_route_first",
    )(q3, k1, k2)


PEER_CHUNK_KEYS = 2


def _gate_tile_bits(n, keys, e1_ref, e2_ref, g_ref):
    first = jnp.where(keys == e1_ref[pl.ds(n, 1), :], g_ref[pl.ds(n, 1), :], 0.0).astype(BF16)
    second = jnp.where(keys == e2_ref[pl.ds(n, 1), :], 1.0, 0.0).astype(BF16)
    c = lax.dot_general(first, second, _NT, preferred_element_type=F32)
    return lax.bitcast_convert_type(c.astype(BF16).astype(F32), jnp.uint32)


def _peer_kernel(xn_ref, qn_ref, e1f_ref, e2f_ref, gf_ref, k1_ref, k2_ref, u0_ref, ua_ref, ub_ref, v_ref, x1_ref,
                 nf_ref, y_ref, c_ref, hid_a_ref, hid_b_ref, e1_ref, e2_ref, g_ref, e1n_ref, e2n_ref, gn_ref, *, tn):
    i = pl.program_id(0)
    j = pl.program_id(1)
    n_steps = pl.num_programs(1)
    half = tn // 2
    lane_blocks = tn // LANES
    ck = PEER_CHUNK_KEYS
    high = jnp.uint32(0xFFFF0000)

    @pl.when(j == 0)
    def _():
        @pl.when(i == 0)
        def _():
            e1_ref[...] = e1f_ref[...]
            e2_ref[...] = e2f_ref[...]
            g_ref[...] = gf_ref[...]

        y_ref[...] = jnp.zeros_like(y_ref)
        keys = lax.broadcasted_iota(jnp.int32, (N_KEYS, LANES), 0).astype(F32)

        def build(p, carry):
            bits = _gate_tile_bits(p, keys, e1_ref, e2_ref, g_ref)
            bits = bits | (_gate_tile_bits(p + half, keys, e1_ref, e2_ref, g_ref) >> 16)
            c_ref[pl.ds(pl.multiple_of(p * C_PITCH, 4), N_KEYS), :] = bits
            return carry

        lax.fori_loop(0, half, build, 0, unroll=8)
        hid_a_ref[...] = lax.dot_general(xn_ref[...], u0_ref[...], _NT, preferred_element_type=F32)

    def weighted(hid_ref, first_key):
        parts = []
        for t in range(ck):
            bits = c_ref[pl.ds(first_key + t, half, stride=C_PITCH), :]
            c = jnp.concatenate([lax.bitcast_convert_type(bits & high, F32),
                                 lax.bitcast_convert_type(bits << 16, F32)], axis=0)
            parts.append((c * jax.nn.gelu(hid_ref[:, t * N_KEYS:(t + 1) * N_KEYS])).astype(BF16))
        return parts

    head = (j // lane_blocks) % PEER_HEADS
    lane_block = j % lane_blocks
    tok0 = pl.multiple_of(lane_block * LANES, LANES)
    xn = xn_ref[...]
    rows = ck * N_KEYS
    w_a = jnp.concatenate(weighted(hid_a_ref, 2 * ck * j), axis=1)
    side1 = _route_side(qn_ref[2 * head, pl.ds(tok0, LANES), :], k1_ref[...])
    hid_b_ref[...] = lax.dot_general(xn, ua_ref[...], _NT, preferred_element_type=F32)
    side2 = _route_side(qn_ref[2 * head + 1, pl.ds(tok0, LANES), :], k2_ref[...])
    y_ref[...] += jnp.dot(w_a, v_ref[0:rows, :], preferred_element_type=F32)
    w_b = jnp.concatenate(weighted(hid_b_ref, 2 * ck * j + ck), axis=1)
    e1n, e2n, gn = _route_combine(side1, side2)
    hid_a_ref[...] = lax.dot_general(xn, ub_ref[...], _NT, preferred_element_type=F32)
    y_ref[...] += jnp.dot(w_b, v_ref[rows:2 * rows, :], preferred_element_type=F32)

    slot0 = pl.multiple_of(head * PEER_TOPK, PEER_TOPK)
    e1n_ref[lane_block, pl.ds(slot0, PEER_TOPK), :] = e1n
    e2n_ref[lane_block, pl.ds(slot0, PEER_TOPK), :] = e2n
    gn_ref[lane_block, pl.ds(slot0, PEER_TOPK), :] = gn

    @pl.when(j == n_steps - 1)
    def _():
        x2 = x1_ref[...] + y_ref[...]
        ms = jnp.mean(x2 * x2, axis=-1, keepdims=True)
        y_ref[...] = x2 * lax.rsqrt(ms + RMS_EPS) * nf_ref[...]
        for lb in range(lane_blocks):
            e1_ref[lb * LANES:(lb + 1) * LANES, :] = e1n_ref[lb].T
            e2_ref[lb * LANES:(lb + 1) * LANES, :] = e2n_ref[lb].T
            g_ref[lb * LANES:(lb + 1) * LANES, :] = gn_ref[lb].T


def _peer_experts(xn, q3, e1_first, e2_first, g_first, k1, k2, u, v, x1, norm_final):
    n, d = xn.shape
    tn = e1_first.shape[0]
    rows = PEER_CHUNK_KEYS * N_KEYS
    n_chunks = u.shape[0] // rows
    n_blocks = n // tn
    slots = PEER_HEADS * PEER_TOPK
    assert n_chunks % 2 == 0 and n % tn == 0 and tn % LANES == 0
    assert n_chunks // 2 >= PEER_HEADS * (tn // LANES), "one routing unit per grid step must cover the next block"
    blk = lambda i, j: (i, 0)
    fixed = lambda i, j: (0, 0)
    first = pl.BlockSpec((tn, slots), fixed)
    kernel = functools.partial(_peer_kernel, tn=tn)
    return pl.pallas_call(
        kernel,
        grid=(n_blocks, n_chunks // 2),
        in_specs=[pl.BlockSpec((tn, d), blk, pipeline_mode=pl.Buffered(1)),
                  pl.BlockSpec((q3.shape[0], tn, LANES), lambda i, j: (0, jnp.minimum(i + 1, n_blocks - 1), 0)),
                  first, first, first,
                  pl.BlockSpec(k1.shape, fixed), pl.BlockSpec(k2.shape, fixed),
                  pl.BlockSpec((rows, d), fixed, pipeline_mode=pl.Buffered(1)),
                  pl.BlockSpec((rows, d), lambda i, j: (2 * j + 1, 0)),
                  pl.BlockSpec((rows, d), lambda i, j: (jnp.minimum(2 * j + 2, n_chunks - 1), 0)),
                  pl.BlockSpec((2 * rows, d), lambda i, j: (j, 0)),
                  pl.BlockSpec((tn, d), blk, pipeline_mode=pl.Buffered(1)),
                  pl.BlockSpec((1, d), fixed)],
        out_specs=pl.BlockSpec((tn, d), blk),
        out_shape=jax.ShapeDtypeStruct((n, d), F32),
        scratch_shapes=[pltpu.VMEM((tn // 2 * C_PITCH, LANES), jnp.uint32),
                        pltpu.VMEM((tn, rows), F32), pltpu.VMEM((tn, rows), F32)]
                       + [pltpu.VMEM((tn, slots), F32)] * 3
                       + [pltpu.VMEM((tn // LANES, slots, LANES), F32)] * 3,
        compiler_params=_params("arbitrary", "arbitrary"),
        name="peer_experts",
    )(xn, q3, e1_first, e2_first, g_first, k1, k2, u, u, u, v, x1, norm_final.reshape(1, d))


def kernel(x_prompt, x_sample, cache_sb_k, cache_sb_v, state_conv, state_lru, meta_tokens, norm_mix, norm_ffn, w_in, b_gate, conv_w, conv_b, w_rg_a, b_rg_a, w_rg_x, b_rg_x, lru_lambda, w_proj_attn, w_proj_lru, w_out, w_query, sub_keys, expert_u, expert_v, norm_final):
    depth = w_in.shape[0]
    assert depth == 1, "one layer: the meta tokens' residual stream is never read after the mixer state"
    batch, seq, d = x_prompt.shape
    dec_batch, dec_seq, _ = x_sample.shape
    n_cache = cache_sb_k.shape[2]
    heads = cache_sb_k.shape[3]
    n_frames = batch * seq
    n_sample = dec_batch * dec_seq
    n_meta = meta_tokens.shape[0]
    assert n_meta == N_META and d == heads * LANES

    l = 0
    w_in_b = w_in[l].astype(BF16)
    w_a = w_rg_a[l].astype(BF16)
    w_x = w_rg_x[l].astype(BF16)
    w_pa = w_proj_attn[l].astype(BF16)
    w_pl = w_proj_lru[l].astype(BF16)
    w_o = w_out[l].astype(BF16)
    wq_b = w_query[l].astype(BF16)
    k1 = sub_keys[l, 0].astype(BF16)
    k2 = sub_keys[l, 1].astype(BF16)
    u_b = expert_u[l].astype(BF16)
    v_b = expert_v[l].astype(BF16)

    x_f = x_prompt.reshape(n_frames, d)
    x_s = jnp.concatenate([x_sample.reshape(n_sample, d), meta_tokens.astype(x_sample.dtype)], axis=0)

    def project(x):
        xn = _rmsnorm_cast(x, norm_mix[l])
        dts = (BF16, F32, F32, F32, F32, F32, F32)
        return [_matmul_cols(xn, w_in_b, g, d, dt) for g, dt in enumerate(dts)]

    q_f, k_f, v_f, xr_f, yg_f, gs_f, gl_f = project(x_f)
    q_s, k_s, v_s, xr_s, yg_s, gs_s, gl_s = project(x_s)
    k_meta, v_meta = k_s[n_sample:], v_s[n_sample:]

    lru_w = (conv_w[l], conv_b[l], w_a, b_rg_a[l], w_x, b_rg_x[l], lru_lambda[l])
    _, buf_meta, h_meta = _conv_rglru(
        xr_s[n_sample:], yg_s[n_sample:], jnp.zeros((1, CONV_WIDTH - 1, d), F32), jnp.zeros((1, 1, d), F32),
        *lru_w, n_streams=1, n_rows=n_meta, shared_state=True)
    m_f, buf_f, h_f = _conv_rglru(xr_f, yg_f, buf_meta, h_meta, *lru_w,
                                  n_streams=batch, n_rows=seq, shared_state=True)
    m_s, buf_s, h_s = _conv_rglru(xr_s[:n_sample], yg_s[:n_sample], state_conv[l], state_lru[l][:, None, :],
                                  *lru_w, n_streams=dec_batch, n_rows=dec_seq, shared_state=False)

    attn_f = _sb_attention(q_f, k_f, v_f, k_meta, v_meta, n_streams=batch, n_new=seq, n_cache=n_meta,
                           shared_cache=True, heads=heads)
    attn_s = _sb_attention(q_s[:n_sample], k_s[:n_sample], v_s[:n_sample],
                           cache_sb_k[l].reshape(dec_batch * n_cache, d), cache_sb_v[l].reshape(dec_batch * n_cache, d),
                           n_streams=dec_batch, n_new=dec_seq, n_cache=n_cache, shared_cache=False, heads=heads)

    def finish(attn, m, gs, gl, x):
        x1, xn2 = _mix_project(attn, m, gs, gl, b_gate[l], x, w_pa, w_pl, w_o, norm_ffn[l])
        q3 = _query_proj(xn2, wq_b)
        tn = _row_block(xn2.shape[0], 512, LANES)
        e1, e2, gate = _peer_route_first(q3, k1, k2, tn)
        return _peer_experts(xn2, q3, e1, e2, gate, k1, k2, u_b, v_b, x1, norm_final)

    y_f = finish(attn_f, m_f, gs_f, gl_f, x_f)
    y_s = finish(attn_s, m_s[:n_sample], gs_s[:n_sample], gl_s[:n_sample], x_s[:n_sample])

    def with_meta(meta_rows, frames):
        meta_b = jnp.broadcast_to(meta_rows[None], (batch, n_meta, d))
        full = jnp.concatenate([meta_b, frames.reshape(batch, seq, d)], axis=1)
        return full.reshape(1, batch, n_meta + seq, heads, LANES)

    return (y_f.reshape(batch, seq, d), y_s.reshape(dec_batch, dec_seq, d),
            with_meta(k_meta, k_f), with_meta(v_meta, v_f),
            buf_f[None], h_f.reshape(1, batch, d),
            k_s[:n_sample].reshape(1, dec_batch, dec_seq, heads, LANES),
            v_s[:n_sample].reshape(1, dec_batch, dec_seq, heads, LANES),
            buf_s[None], h_s.reshape(1, dec_batch, d))
```

```python
import functools
import math

import jax
import jax.numpy as jnp
from jax import lax
from jax.experimental import pallas as pl
from jax.experimental.pallas import tpu as pltpu

F32 = jnp.float32
BF16 = jnp.bfloat16

LANES = 128
SUBLANES = 8
VMEM_LIMIT_BYTES = 56 * 1024 * 1024
MXU_COLS = 256

RMS_EPS = 1e-6
RG_C = 8.0
CONV_WIDTH = 4
N_META = 16
PEER_TOPK = 16
PEER_HEADS = 8
N_KEYS = 128
SB_LOG_CUTOFF = -105.0
SB_MASKED = -1e30
SB_CHAINS = 8
C_PITCH = 132

_NT = (((1,), (1,)), ((), ()))


def _params(*sem):
    return pltpu.CompilerParams(dimension_semantics=sem, vmem_limit_bytes=VMEM_LIMIT_BYTES)


def _row_block(n, target, align):
    best = None
    for t in range(align, min(n, target) + 1, align):
        if n % t == 0:
            best = t
    assert best is not None, (n, target, align)
    return best


def _softplus_neg_abs(z):
    return jnp.log1p(jnp.exp(-jnp.abs(z)))


def _split_bf16(x):
    hi = x.astype(BF16)
    lo = (x - hi.astype(F32)).astype(BF16)
    return hi, lo


def _rmsnorm_kernel(x_ref, g_ref, o_ref):
    x = x_ref[...]
    ms = jnp.mean(x * x, axis=-1, keepdims=True)
    o_ref[...] = (x * lax.rsqrt(ms + RMS_EPS) * g_ref[...]).astype(o_ref.dtype)


def _rmsnorm_cast(x, g):
    n, d = x.shape
    tm = _row_block(n, 512, 16)
    return pl.pallas_call(
        _rmsnorm_kernel,
        grid=(n // tm,),
        in_specs=[pl.BlockSpec((tm, d), lambda i: (i, 0)), pl.BlockSpec((1, d), lambda i: (0, 0))],
        out_specs=pl.BlockSpec((tm, d), lambda i: (i, 0)),
        out_shape=jax.ShapeDtypeStruct((n, d), BF16),
        compiler_params=_params("parallel"),
        name="rmsnorm_cast",
    )(x, g.reshape(1, d))


def _matmul_kernel(x_ref, w_ref, o_ref):
    o_ref[...] = jnp.dot(x_ref[...], w_ref[...], preferred_element_type=F32).astype(o_ref.dtype)


def _matmul_cols(x, w, col_block, ncols, out_dtype):
    n, k = x.shape
    tm = _row_block(n, 512, 16)
    return pl.pallas_call(
        _matmul_kernel,
        grid=(n // tm,),
        in_specs=[
            pl.BlockSpec((tm, k), lambda i: (i, 0)),
            pl.BlockSpec((k, ncols), lambda i: (0, col_block), pipeline_mode=pl.Buffered(1)),
        ],
        out_specs=pl.BlockSpec((tm, ncols), lambda i: (i, 0)),
        out_shape=jax.ShapeDtypeStruct((n, ncols), out_dtype),
        compiler_params=_params("parallel"),
        name="in_proj",
    )(x, w)


def _sb_kernel(*refs, n_new, n_cache, q_rows, chains, joint, scale):
    q_ref = refs[0]
    o_ref, kb_ref, vb_ref = refs[-3:]
    head0 = LANES - N_META
    new0 = head0 + n_cache
    n_scratch = kb_ref.shape[0]

    def fill(src_refs, n_rows, dst0):
        def chunk(c, carry):
            src = pl.multiple_of(c * LANES, LANES)
            dst = pl.multiple_of(dst0 + c * LANES, 16)
            for src_ref, dst_ref in zip(src_refs, (kb_ref, vb_ref)):
                dst_ref[pl.ds(dst, LANES), :] = src_ref[pl.ds(src, LANES), :].astype(BF16)
            return carry

        full = n_rows // LANES
        if full:
            lax.fori_loop(0, full, chunk, 0)
        if n_rows % LANES:
            for src_ref, dst_ref in zip(src_refs, (kb_ref, vb_ref)):
                dst_ref[dst0 + full * LANES:dst0 + n_rows, :] = src_ref[full * LANES:n_rows, :].astype(BF16)

    zeros_head = jnp.zeros((head0, LANES), BF16)
    kb_ref[0:head0, :] = zeros_head
    vb_ref[0:head0, :] = zeros_head
    if joint:
        fill(refs[1:3], n_cache + n_new, head0)
    else:
        fill(refs[3:5], n_cache, head0)
        fill(refs[1:3], n_new, new0)
    if new0 + n_new < n_scratch:
        tail = jnp.zeros((n_scratch - new0 - n_new, LANES), BF16)
        kb_ref[new0 + n_new:n_scratch, :] = tail
        vb_ref[new0 + n_new:n_scratch, :] = tail

    row = lax.broadcasted_iota(jnp.int32, (q_rows, LANES), 0)
    col = lax.broadcasted_iota(jnp.int32, (q_rows, LANES), 1)
    kr = lax.broadcasted_iota(jnp.int32, (LANES, LANES), 0)
    kc = lax.broadcasted_iota(jnp.int32, (LANES, LANES), 1)
    later = jnp.concatenate([jnp.where(kr > kc, 1.0, 0.0), jnp.ones((LANES, LANES), F32)], axis=1).astype(BF16)

    def segments(qbs, blocks, biases, carries, accs):
        zs, sps, logs = [], [], []
        for qb, blk, bias in zip(qbs, blocks, biases):
            s0 = pl.multiple_of(blk * LANES, LANES)
            z = lax.dot_general(qb, kb_ref[pl.ds(s0, LANES), :], _NT, preferred_element_type=F32) * scale + bias
            sp = jnp.log(1.0 + jnp.exp(-jnp.abs(z)))
            zs.append(z)
            sps.append(sp)
            logs.append(-(jnp.maximum(z, 0.0) + sp))
        hi, lo = _split_bf16(jnp.concatenate(logs, axis=0))
        sums = jnp.dot(jnp.concatenate([hi, lo], axis=0), later, preferred_element_type=F32)
        sums = sums[:chains * q_rows] + sums[chains * q_rows:]
        new_carries, new_accs = [], []
        for c in range(chains):
            s = sums[c * q_rows:(c + 1) * q_rows]
            w = jnp.exp(jnp.minimum(zs[c], 0.0) - sps[c] + (s[:, :LANES] + carries[c]))
            v0 = pl.multiple_of(blocks[c] * LANES, LANES)
            new_accs.append(accs[c] + jnp.dot(w.astype(BF16), vb_ref[pl.ds(v0, LANES), :],
                                              preferred_element_type=F32))
            new_carries.append(carries[c] + s[:, LANES:])
        return tuple(new_carries), tuple(new_accs)

    causal_bias = jnp.where(col < row, 0.0, SB_MASKED)
    not_first_positions = jnp.where(col >= LANES - N_META, 0.0, 1.0)

    def query_group(g, carry_unused):
        q0s = [pl.multiple_of((g * chains + c) * q_rows, q_rows) for c in range(chains)]
        qbs = [q_ref[pl.ds(q0, q_rows), :] for q0 in q0s]
        diag = [(new0 + (g * chains + c) * q_rows) // LANES for c in range(chains)]
        zero = jnp.zeros((q_rows, LANES), F32)
        carries, accs = segments(qbs, diag, [causal_bias] * chains, (zero,) * chains, (zero,) * chains)

        def cond(state):
            dist, carries, _ = state
            go = jnp.bool_(False)
            for c in range(chains):
                go = jnp.logical_or(go, jnp.logical_and(diag[c] - dist >= 0, jnp.max(carries[c]) > SB_LOG_CUTOFF))
            return go

        def body(state):
            dist, carries, accs = state
            blocks = [jnp.maximum(diag[c] - dist, 0) for c in range(chains)]
            biases = [not_first_positions * jnp.where(diag[c] - dist == 0, SB_MASKED, 0.0)
                      + jnp.where(diag[c] - dist < 0, SB_MASKED, 0.0) for c in range(chains)]
            carries, accs = segments(qbs, blocks, biases, carries, accs)
            return dist + 1, carries, accs

        _, _, accs = lax.while_loop(cond, body, (jnp.int32(1), carries, accs))
        for c in range(chains):
            o_ref[pl.ds(q0s[c], q_rows), :] = accs[c].astype(o_ref.dtype)
        return carry_unused

    lax.fori_loop(0, n_new // (q_rows * chains), query_group, 0)


def _sb_attention(q, keys, values, *, n_streams, n_new, n_cache, heads):
    assert (n_cache - N_META) % LANES == 0
    joint = not isinstance(keys, tuple)
    q_rows = min(n_new, LANES)
    assert n_new % q_rows == 0 and q_rows % 16 == 0
    new0 = LANES + (n_cache - N_META)
    n_scratch = new0 + (n_new // q_rows - 1) * q_rows + LANES
    stream_head = lambda b, h: (b, h)
    new_spec = pl.BlockSpec((n_new, LANES), stream_head)
    if joint:
        kv_specs = [pl.BlockSpec((n_cache + n_new, LANES), stream_head)] * 2
        kv_args = (keys, values)
    else:
        kv_specs = [new_spec, new_spec] + [pl.BlockSpec((n_cache, LANES), stream_head)] * 2
        kv_args = (keys[0], values[0], keys[1], values[1])
    n_qblocks = n_new // q_rows
    chains = SB_CHAINS if n_qblocks % SB_CHAINS == 0 else 1
    kernel = functools.partial(_sb_kernel, n_new=n_new, n_cache=n_cache, q_rows=q_rows, chains=chains, joint=joint,
                               scale=1.0 / math.sqrt(LANES))
    return pl.pallas_call(
        kernel,
        grid=(n_streams, heads),
        in_specs=[new_spec] + kv_specs,
        out_specs=new_spec,
        out_shape=jax.ShapeDtypeStruct(q.shape, BF16),
        scratch_shapes=[pltpu.VMEM((n_scratch, LANES), BF16), pltpu.VMEM((n_scratch, LANES), BF16)],
        compiler_params=_params("parallel", "parallel"),
        name="sb_attention",
    )(q, *kv_args)


def _lru_kernel(xr_ref, yg_ref, buf_ref, h0_ref, cw_ref, cb_ref, wa_ref, ba_ref, wx_ref, bx_ref, lam_ref,
                m_ref, nbuf_ref, hl_ref, xp_ref, h_ref, *, n_rows, chunk):
    pad = SUBLANES
    n_hist = CONV_WIDTH - 1
    xp_ref[pad - n_hist:pad, :] = buf_ref[0]

    def copy(c, carry):
        r0 = pl.multiple_of(c * chunk, chunk)
        xp_ref[pl.ds(r0 + pad, chunk), :] = xr_ref[pl.ds(r0, chunk), :]
        return carry

    lax.fori_loop(0, n_rows // chunk, copy, 0)
    nbuf_ref[0] = xp_ref[n_rows + pad - n_hist:n_rows + pad, :]

    lam = lam_ref[...]
    sp_lam = jnp.maximum(-lam, 0.0) + _softplus_neg_abs(lam)
    cw = cw_ref[...]
    cb = cb_ref[...]
    wa = wa_ref[0]
    wx = wx_ref[0]
    ba = ba_ref[...]
    bx = bx_ref[...]
    groups = chunk // SUBLANES
    sub = lax.broadcasted_iota(jnp.int32, (groups, SUBLANES, LANES), 1)
    h_ref[...] = h0_ref[0]

    def step(c, carry):
        r0 = pl.multiple_of(c * chunk, chunk)
        xc = cb
        for tap in range(CONV_WIDTH):
            xc = xc + cw[tap:tap + 1, :] * xp_ref[pl.ds(r0 + pad - n_hist + tap, chunk), :]
        xcb = xc.astype(BF16)
        r = jax.nn.sigmoid(jnp.dot(xcb, wa, preferred_element_type=F32) + ba)
        gate_i = jax.nn.sigmoid(jnp.dot(xcb, wx, preferred_element_type=F32) + bx)
        log_a = -RG_C * r * sp_lam
        a = jnp.exp(log_a)
        t = jnp.tanh(log_a)
        u = jnp.sqrt(-2.0 * t / (1.0 - t)) * (gate_i * xc)
        a3 = a.reshape(groups, SUBLANES, LANES)
        u3 = u.reshape(groups, SUBLANES, LANES)
        shift = 1
        while shift < SUBLANES:
            keep = sub >= shift
            u3 = jnp.where(keep, a3 * pltpu.roll(u3, shift, axis=1) + u3, u3)
            a3 = jnp.where(keep, a3 * pltpu.roll(a3, shift, axis=1), a3)
            shift *= 2
        h = h_ref[...]
        rows = []
        for g in range(groups):
            hg = a3[g] * h + u3[g]
            h = hg[SUBLANES - 1:SUBLANES, :]
            rows.append(hg)
        h_ref[...] = h
        hs = jnp.concatenate(rows, axis=0)
        m_ref[pl.ds(r0, chunk), :] = (jax.nn.gelu(yg_ref[pl.ds(r0, chunk), :]) * hs).astype(m_ref.dtype)
        return carry

    n_chunks = n_rows // chunk
    lax.fori_loop(0, n_chunks, step, 0, unroll=2 if n_chunks % 2 == 0 else 1)
    hl_ref[0] = h_ref[...]


def _conv_rglru(xr, yg, buf, h0, conv_w, conv_b, w_a, b_a, w_x, b_x, lam, *, n_streams, n_rows, shared_state):
    width = xr.shape[1]
    heads = width // LANES
    chunk = _row_block(n_rows, LANES, 16)
    state_map = (lambda b, c: (0, 0, c)) if shared_state else (lambda b, c: (b, 0, c))
    seq_spec = pl.BlockSpec((n_rows, LANES), lambda b, c: (b, c))
    vec_spec = pl.BlockSpec((1, LANES), lambda b, c: (0, c))
    gate_spec = pl.BlockSpec((1, LANES, LANES), lambda b, c: (c, 0, 0))
    kernel = functools.partial(_lru_kernel, n_rows=n_rows, chunk=chunk)
    return pl.pallas_call(
        kernel,
        grid=(n_streams, heads),
        in_specs=[seq_spec, seq_spec,
                  pl.BlockSpec((1, CONV_WIDTH - 1, LANES), state_map), pl.BlockSpec((1, 1, LANES), state_map),
                  pl.BlockSpec((CONV_WIDTH, LANES), lambda b, c: (0, c)), vec_spec,
                  gate_spec, vec_spec, gate_spec, vec_spec, vec_spec],
        out_specs=[seq_spec,
                   pl.BlockSpec((1, CONV_WIDTH - 1, LANES), lambda b, c: (b, 0, c)),
                   pl.BlockSpec((1, 1, LANES), lambda b, c: (b, 0, c))],
        out_shape=[jax.ShapeDtypeStruct(xr.shape, BF16),
                   jax.ShapeDtypeStruct((n_streams, CONV_WIDTH - 1, width), F32),
                   jax.ShapeDtypeStruct((n_streams, 1, width), F32)],
        scratch_shapes=[pltpu.VMEM((n_rows + SUBLANES, LANES), F32), pltpu.VMEM((1, LANES), F32)],
        compiler_params=_params("parallel", "parallel"),
        name="conv_rglru",
    )(xr, yg, buf, h0, conv_w, conv_b.reshape(1, width), w_a, b_a.reshape(1, width), w_x, b_x.reshape(1, width),
      lam.reshape(1, width))


def _mix_kernel(attn_ref, m_ref, gs_ref, gl_ref, bgs_ref, bgl_ref, x_ref, wpa_ref, wpl_ref, wo_ref, nf_ref,
                x1_ref, xn_ref):
    y_sb = jnp.dot(attn_ref[...], wpa_ref[...], preferred_element_type=F32)
    y_lru = jnp.dot(m_ref[...], wpl_ref[...], preferred_element_type=F32)
    g_sb = jax.nn.sigmoid(gs_ref[...] + bgs_ref[...])
    g_lru = jax.nn.sigmoid(gl_ref[...] + bgl_ref[...])
    mix = (g_sb * y_sb + g_lru * y_lru).astype(BF16)
    x1 = x_ref[...] + jnp.dot(mix, wo_ref[...], preferred_element_type=F32)
    x1_ref[...] = x1
    ms = jnp.mean(x1 * x1, axis=-1, keepdims=True)
    xn_ref[...] = (x1 * lax.rsqrt(ms + RMS_EPS) * nf_ref[...]).astype(xn_ref.dtype)


def _mix_project(attn, m, gs, gl, b_gate, x, w_pa, w_pl, w_o, norm_ffn):
    n, d = x.shape
    tm = _row_block(n, 256, 16)
    row = lambda i: (i, 0)
    fixed = lambda i: (0, 0)
    blk = pl.BlockSpec((tm, d), row)
    vec = pl.BlockSpec((1, d), fixed)
    wspec = pl.BlockSpec((d, d), fixed, pipeline_mode=pl.Buffered(1))
    return pl.pallas_call(
        _mix_kernel,
        grid=(n // tm,),
        in_specs=[blk, blk, blk, blk, vec, vec, blk, wspec, wspec, wspec, vec],
        out_specs=[blk, blk],
        out_shape=[jax.ShapeDtypeStruct((n, d), F32), jax.ShapeDtypeStruct((n, d), BF16)],
        compiler_params=_params("parallel"),
        name="mix_project",
    )(attn, m, gs, gl, b_gate[:d].reshape(1, d), b_gate[d:].reshape(1, d), x, w_pa, w_pl, w_o,
      norm_ffn.reshape(1, d))


def _top_rows(s, k):
    n = s.shape[0]
    rows = lax.broadcasted_iota(jnp.int32, s.shape, 0)
    vals, idxs = [], []
    for _ in range(k):
        m = jnp.max(s, axis=0, keepdims=True)
        idx = jnp.min(jnp.where(s == m, rows, n), axis=0, keepdims=True)
        s = jnp.where(rows == idx, -jnp.inf, s)
        vals.append(m)
        idxs.append(idx)
        yield
    return jnp.concatenate(vals, axis=0), jnp.concatenate(idxs, axis=0)


def _finish(stages):
    while True:
        try:
            next(stages)
        except StopIteration as done:
            return done.value


def _take_rows(table, idx):
    out = jnp.zeros(idx.shape, table.dtype)
    for r in range(table.shape[0]):
        out = jnp.where(idx == r, table[r:r + 1, :], out)
    return out


def _route_unit(q1, q2, k1, k2):
    t1, i1 = yield from _top_rows(lax.dot_general(k1, q1, _NT, preferred_element_type=F32), PEER_TOPK)
    t2, i2 = yield from _top_rows(lax.dot_general(k2, q2, _NT, preferred_element_type=F32), PEER_TOPK)
    cand = jnp.concatenate(
        [t1[0:1, :] + t2]
        + [t1[a:a + 1, :] + t2[0:8, :] for a in range(1, 4)]
        + [t1[a:a + 1, :] + t2[0:4, :] for a in range(4, 8)]
        + [t1[8:16, :] + t2[0:1, :]], axis=0)
    score, r = yield from _top_rows(cand, PEER_TOPK)
    rank1 =jnp.where(r < 16, 0, jnp.where(r < 40, 1 + lax.shift_right_logical(r - 16, 3),
                                           jnp.where(r < 56, 4 + lax.shift_right_logical(r - 40, 2), r - 48)))
    rank2 = jnp.where(r < 16, r, jnp.where(r < 40, jnp.bitwise_and(r - 16, 7),
                                           jnp.where(r < 56, jnp.bitwise_and(r - 40, 3), 0)))
    e1 = _take_rows(i1, rank1)
    e2 = _take_rows(i2, rank2)
    ex = jnp.exp(score - jnp.max(score, axis=0, keepdims=True))
    gate = ex / jnp.sum(ex, axis=0, keepdims=True)
    return e1.astype(F32), e2.astype(F32), gate


def _query_kernel(x_ref, w_ref, o_ref):
    q = jnp.dot(x_ref[...], w_ref[...], preferred_element_type=F32).astype(o_ref.dtype)
    for part in range(o_ref.shape[0]):
        o_ref[part] = q[:, part * LANES:(part + 1) * LANES]


def _query_proj(xn, wq):
    n, d = xn.shape
    parts = wq.shape[1] // LANES
    tm = _row_block(n, 512, 16)
    return pl.pallas_call(
        _query_kernel,
        grid=(n // tm,),
        in_specs=[pl.BlockSpec((tm, d), lambda i: (i, 0)),
                  pl.BlockSpec(wq.shape, lambda i: (0, 0), pipeline_mode=pl.Buffered(1))],
        out_specs=pl.BlockSpec((parts, tm, LANES), lambda i: (0, i, 0)),
        out_shape=jax.ShapeDtypeStruct((parts, n, LANES), BF16),
        compiler_params=_params("parallel"),
        name="query_proj",
    )(xn, wq)


def _route_kernel(q_ref, k1_ref, k2_ref, e1_ref, e2_ref, g_ref, e1s_ref, e2s_ref, gs_ref):
    k1 = k1_ref[...]
    k2 = k2_ref[...]

    def head(h, carry):
        e1, e2, gate = _finish(_route_unit(q_ref[2 * h], q_ref[2 * h + 1], k1, k2))
        out0 = pl.multiple_of(h * PEER_TOPK, PEER_TOPK)
        e1s_ref[pl.ds(out0, PEER_TOPK), :] = e1
        e2s_ref[pl.ds(out0, PEER_TOPK), :] = e2
        gs_ref[pl.ds(out0, PEER_TOPK), :] = gate
        return carry

    lax.fori_loop(0, PEER_HEADS, head, 0)
    e1_ref[...] = e1s_ref[...].T
    e2_ref[...] = e2s_ref[...].T
    g_ref[...] = gs_ref[...].T


def _peer_route_first(q3, k1, k2, n_first):
    parts = q3.shape[0]
    slots = PEER_HEADS * PEER_TOPK
    fixed = lambda i: (0, 0)
    out = pl.BlockSpec((LANES, slots), lambda i: (i, 0))
    return pl.pallas_call(
        _route_kernel,
        grid=(n_first // LANES,),
        in_specs=[pl.BlockSpec((parts, LANES, LANES), lambda i: (0, i, 0)),
                  pl.BlockSpec(k1.shape, fixed), pl.BlockSpec(k2.shape, fixed)],
        out_specs=[out, out, out],
        out_shape=[jax.ShapeDtypeStruct((n_first, slots), F32)] * 3,
        scratch_shapes=[pltpu.VMEM((slots, LANES), F32)] * 3,
        compiler_params=_params("parallel"),
        name="peer_route_first",
    )(q3, k1, k2)


PEER_CHUNK_KEYS = 2


def _gate_tile_bits(n, keys, e1_ref, e2_ref, g_ref):
    first = jnp.where(keys == e1_ref[pl.ds(n, 1), :], g_ref[pl.ds(n, 1), :], 0.0).astype(BF16)
    second = jnp.where(keys == e2_ref[pl.ds(n, 1), :], 1.0, 0.0).astype(BF16)
    c = lax.dot_general(first, second, _NT, preferred_element_type=F32)
    return lax.bitcast_convert_type(c.astype(BF16).astype(F32), jnp.uint32)


def _peer_kernel(xn_ref, qn_ref, e1f_ref, e2f_ref, gf_ref, k1_ref, k2_ref, u0_ref, ua_ref, ub_ref, v_ref, x1_ref,
                 nf_ref, y_ref, c_ref, hid_a_ref, hid_b_ref, e1_ref, e2_ref, g_ref, e1n_ref, e2n_ref, gn_ref, *, tn):
    i = pl.program_id(0)
    j = pl.program_id(1)
    n_steps = pl.num_programs(1)
    half = tn // 2
    lane_blocks = tn // LANES
    ck = PEER_CHUNK_KEYS
    high = jnp.uint32(0xFFFF0000)

    @pl.when(j == 0)
    def _():
        @pl.when(i == 0)
        def _():
            e1_ref[...] = e1f_ref[...]
            e2_ref[...] = e2f_ref[...]
            g_ref[...] = gf_ref[...]

        y_ref[...] = jnp.zeros_like(y_ref)
        keys = lax.broadcasted_iota(jnp.int32, (N_KEYS, LANES), 0).astype(F32)

        def build(p, carry):
            bits = _gate_tile_bits(p, keys, e1_ref, e2_ref, g_ref)
            bits = bits | (_gate_tile_bits(p + half, keys, e1_ref, e2_ref, g_ref) >> 16)
            c_ref[pl.ds(pl.multiple_of(p * C_PITCH, 4), N_KEYS), :] = bits
            return carry

        lax.fori_loop(0, half, build, 0, unroll=32)
        hid_a_ref[...] = lax.dot_general(xn_ref[...], u0_ref[...], _NT, preferred_element_type=F32)

    def weighted(hid_ref, first_key):
        parts = []
        for t in range(ck):
            bits = c_ref[pl.ds(first_key + t, half, stride=C_PITCH), :]
            c = jnp.concatenate([lax.bitcast_convert_type(bits & high, F32),
                                 lax.bitcast_convert_type(bits << 16, F32)], axis=0)
            parts.append((c * jax.nn.gelu(hid_ref[:, t * N_KEYS:(t + 1) * N_KEYS])).astype(BF16))
        return parts

    head = (j // lane_blocks) % PEER_HEADS
    lane_block = j % lane_blocks
    tok0 = pl.multiple_of(lane_block * LANES, LANES)
    xn = xn_ref[...]
    rows = ck * N_KEYS
    routing = _route_unit(qn_ref[2 * head, pl.ds(tok0, LANES), :], qn_ref[2 * head + 1, pl.ds(tok0, LANES), :],
                          k1_ref[...], k2_ref[...])

    def route_some(count):
        for _ in range(count):
            next(routing, None)

    def add_weighted(w, v_rows, extractions):
        for col in range(0, y_ref.shape[1], MXU_COLS):
            y_ref[:, col:col + MXU_COLS] += jnp.dot(w, v_ref[v_rows:v_rows + rows, col:col + MXU_COLS],
                                                    preferred_element_type=F32)
            route_some(extractions)

    w_a = jnp.concatenate(weighted(hid_a_ref, 2 * ck * j), axis=1)
    hid_b_ref[...] = lax.dot_general(xn, ua_ref[...], _NT, preferred_element_type=F32)
    route_some(4)
    add_weighted(w_a, 0, 2)
    w_b = jnp.concatenate(weighted(hid_b_ref, 2 * ck * j + ck), axis=1)
    hid_a_ref[...] = lax.dot_general(xn, ub_ref[...], _NT, preferred_element_type=F32)
    route_some(4)
    add_weighted(w_b, rows, 3)
    e1n, e2n, gn = _finish(routing)

    slot0 = pl.multiple_of(head * PEER_TOPK, PEER_TOPK)
    e1n_ref[lane_block, pl.ds(slot0, PEER_TOPK), :] = e1n
    e2n_ref[lane_block, pl.ds(slot0, PEER_TOPK), :] = e2n
    gn_ref[lane_block, pl.ds(slot0, PEER_TOPK), :] = gn

    @pl.when(j == n_steps - 1)
    def _():
        x2 = x1_ref[...] + y_ref[...]
        ms = jnp.mean(x2 * x2, axis=-1, keepdims=True)
        y_ref[...] = x2 * lax.rsqrt(ms + RMS_EPS) * nf_ref[...]
        for lb in range(lane_blocks):
            e1_ref[lb * LANES:(lb + 1) * LANES, :] = e1n_ref[lb].T
            e2_ref[lb * LANES:(lb + 1) * LANES, :] = e2n_ref[lb].T
            g_ref[lb * LANES:(lb + 1) * LANES, :] = gn_ref[lb].T


def _peer_experts(xn, q3, e1_first, e2_first, g_first, k1, k2, u, v, x1, norm_final):
    n, d = xn.shape
    tn = e1_first.shape[0]
    rows = PEER_CHUNK_KEYS * N_KEYS
    n_chunks = u.shape[0] // rows
    n_blocks = n // tn
    slots = PEER_HEADS * PEER_TOPK
    assert n_chunks % 2 == 0 and n % tn == 0 and tn % LANES == 0
    assert n_chunks // 2 >= PEER_HEADS * (tn // LANES), "one routing unit per grid step must cover the next block"
    blk = lambda i, j: (i, 0)
    fixed = lambda i, j: (0, 0)
    first = pl.BlockSpec((tn, slots), fixed)
    kernel = functools.partial(_peer_kernel, tn=tn)
    return pl.pallas_call(
        kernel,
        grid=(n_blocks, n_chunks // 2),
        in_specs=[pl.BlockSpec((tn, d), blk, pipeline_mode=pl.Buffered(1)),
                  pl.BlockSpec((q3.shape[0], tn, LANES), lambda i, j: (0, jnp.minimum(i + 1, n_blocks - 1), 0)),
                  first, first, first,
                  pl.BlockSpec(k1.shape, fixed), pl.BlockSpec(k2.shape, fixed),
                  pl.BlockSpec((rows, d), fixed, pipeline_mode=pl.Buffered(1)),
                  pl.BlockSpec((rows, d), lambda i, j: (2 * j + 1, 0)),
                  pl.BlockSpec((rows, d), lambda i, j: (jnp.minimum(2 * j + 2, n_chunks - 1), 0)),
                  pl.BlockSpec((2 * rows, d), lambda i, j: (j, 0)),
                  pl.BlockSpec((tn, d), blk, pipeline_mode=pl.Buffered(1)),
                  pl.BlockSpec((1, d), fixed)],
        out_specs=pl.BlockSpec((tn, d), blk),
        out_shape=jax.ShapeDtypeStruct((n, d), F32),
        scratch_shapes=[pltpu.VMEM((tn // 2 * C_PITCH, LANES), jnp.uint32),
                        pltpu.VMEM((tn, rows), F32), pltpu.VMEM((tn, rows), F32)]
                       + [pltpu.VMEM((tn, slots), F32)] * 3
                       + [pltpu.VMEM((tn // LANES, slots, LANES), F32)] * 3,
        compiler_params=_params("arbitrary", "arbitrary"),
        name="peer_experts",
    )(xn, q3, e1_first, e2_first, g_first, k1, k2, u, u, u, v, x1, norm_final.reshape(1, d))


def kernel(x_prompt, x_sample, cache_sb_k, cache_sb_v, state_conv, state_lru, meta_tokens, norm_mix, norm_ffn, w_in, b_gate, conv_w, conv_b, w_rg_a, b_rg_a, w_rg_x, b_rg_x, lru_lambda, w_proj_attn, w_proj_lru, w_out, w_query, sub_keys, expert_u, expert_v, norm_final):
    depth = w_in.shape[0]
    assert depth == 1, "one layer: the meta tokens' residual stream is never read after the mixer state"
    batch, seq, d = x_prompt.shape
    dec_batch, dec_seq, _ = x_sample.shape
    n_cache = cache_sb_k.shape[2]
    heads = cache_sb_k.shape[3]
    n_frames = batch * seq
    n_sample = dec_batch * dec_seq
    n_meta = meta_tokens.shape[0]
    assert n_meta == N_META and d == heads * LANES

    l = 0
    w_in_b = w_in[l].astype(BF16)
    w_a = w_rg_a[l].astype(BF16)
    w_x = w_rg_x[l].astype(BF16)
    w_pa = w_proj_attn[l].astype(BF16)
    w_pl = w_proj_lru[l].astype(BF16)
    w_o = w_out[l].astype(BF16)
    wq_b = w_query[l].astype(BF16)
    k1 = sub_keys[l, 0].astype(BF16)
    k2 = sub_keys[l, 1].astype(BF16)
    u_b = expert_u[l].astype(BF16)
    v_b = expert_v[l].astype(BF16)

    x_f = x_prompt.reshape(n_frames, d)
    x_s = jnp.concatenate([x_sample.reshape(n_sample, d), meta_tokens.astype(x_sample.dtype)], axis=0)

    col_q, col_k, col_v = 0, 1, 2
    other_cols = (3, 4, 5, 6)

    def project(xn, cols, dtype):
        return [_matmul_cols(xn, w_in_b, g, d, dtype) for g in cols]

    xn_f = _rmsnorm_cast(x_f, norm_mix[l])
    xn_s = _rmsnorm_cast(x_s, norm_mix[l])
    (q_f,), (q_s,) = project(xn_f, (col_q,), BF16), project(xn_s, (col_q,), BF16)
    xr_f, yg_f, gs_f, gl_f = project(xn_f, other_cols, F32)
    k_s, v_s, xr_s, yg_s, gs_s, gl_s = project(xn_s, (col_k, col_v) + other_cols, F32)
    xn_meta = jnp.broadcast_to(xn_s[n_sample:][None], (batch, n_meta, d))
    xn_p = jnp.concatenate([xn_meta, xn_f.reshape(batch, seq, d)], axis=1).reshape(batch * (n_meta + seq), d)
    k_p, v_p = project(xn_p, (col_k, col_v), F32)

    lru_w = (conv_w[l], conv_b[l], w_a, b_rg_a[l], w_x, b_rg_x[l], lru_lambda[l])
    _, buf_meta, h_meta = _conv_rglru(
        xr_s[n_sample:], yg_s[n_sample:], jnp.zeros((1, CONV_WIDTH - 1, d), F32), jnp.zeros((1, 1, d), F32),
        *lru_w, n_streams=1, n_rows=n_meta, shared_state=True)
    m_f, buf_f, h_f = _conv_rglru(xr_f, yg_f, buf_meta, h_meta, *lru_w,
                                  n_streams=batch, n_rows=seq, shared_state=True)
    m_s, buf_s, h_s = _conv_rglru(xr_s[:n_sample], yg_s[:n_sample], state_conv[l], state_lru[l][:, None, :],
                                  *lru_w, n_streams=dec_batch, n_rows=dec_seq, shared_state=False)

    attn_f = _sb_attention(q_f, k_p, v_p, n_streams=batch, n_new=seq, n_cache=n_meta, heads=heads)
    attn_s = _sb_attention(q_s[:n_sample],
                           (k_s[:n_sample], cache_sb_k[l].reshape(dec_batch * n_cache, d)),
                           (v_s[:n_sample], cache_sb_v[l].reshape(dec_batch * n_cache, d)),
                           n_streams=dec_batch, n_new=dec_seq, n_cache=n_cache, heads=heads)

    def finish(attn, m, gs, gl, x):
        x1, xn2 = _mix_project(attn, m, gs, gl, b_gate[l], x, w_pa, w_pl, w_o, norm_ffn[l])
        q3 = _query_proj(xn2, wq_b)
        tn = _row_block(xn2.shape[0], 512, LANES)
        e1, e2, gate = _peer_route_first(q3, k1, k2, tn)
        return _peer_experts(xn2, q3, e1, e2, gate, k1, k2, u_b, v_b, x1, norm_final)

    y_f = finish(attn_f, m_f, gs_f, gl_f, x_f)
    y_s = finish(attn_s, m_s[:n_sample], gs_s[:n_sample], gl_s[:n_sample], x_s[:n_sample])

    return (y_f.reshape(batch, seq, d), y_s.reshape(dec_batch, dec_seq, d),
            k_p.reshape(1, batch, n_meta + seq, heads, LANES), v_p.reshape(1, batch, n_meta + seq, heads, LANES),
            buf_f[None], h_f.reshape(1, batch, d),
            k_s[:n_sample].reshape(1, dec_batch, dec_seq, heads, LANES),
            v_s[:n_sample].reshape(1, dec_batch, dec_seq, heads, LANES),
            buf_s[None], h_s.reshape(1, dec_batch, d))
```

```python
import functools
import math

import jax
import jax.numpy as jnp
from jax import lax
from jax.experimental import pallas as pl
from jax.experimental.pallas import tpu as pltpu

F32 = jnp.float32
BF16 = jnp.bfloat16

LANES = 128
SUBLANES = 8
VMEM_LIMIT_BYTES = 56 * 1024 * 1024
MXU_COLS = 256

RMS_EPS = 1e-6
RG_C = 8.0
CONV_WIDTH = 4
N_META = 16
PEER_TOPK = 16
PEER_HEADS = 8
N_KEYS = 128
SB_LOG_CUTOFF = -105.0
SB_MASKED = -1e30
SB_CHAINS = 8
SB_FIRST_BLOCKS = 3
SB_LOOP_BLOCKS = 2
C_PITCH = 132

_NT = (((1,), (1,)), ((), ()))


def _params(*sem):
    return pltpu.CompilerParams(dimension_semantics=sem, vmem_limit_bytes=VMEM_LIMIT_BYTES)


def _row_block(n, target, align):
    best = None
    for t in range(align, min(n, target) + 1, align):
        if n % t == 0:
            best = t
    assert best is not None, (n, target, align)
    return best


def _softplus_neg_abs(z):
    return jnp.log1p(jnp.exp(-jnp.abs(z)))


def _split_bf16(x):
    hi = x.astype(BF16)
    lo = (x - hi.astype(F32)).astype(BF16)
    return hi, lo


def _rmsnorm_kernel(x_ref, g_ref, o_ref):
    x = x_ref[...]
    ms = jnp.mean(x * x, axis=-1, keepdims=True)
    o_ref[...] = (x * lax.rsqrt(ms + RMS_EPS) * g_ref[...]).astype(o_ref.dtype)


def _rmsnorm_cast(x, g):
    n, d = x.shape
    tm = _row_block(n, 512, 16)
    return pl.pallas_call(
        _rmsnorm_kernel,
        grid=(n // tm,),
        in_specs=[pl.BlockSpec((tm, d), lambda i: (i, 0)), pl.BlockSpec((1, d), lambda i: (0, 0))],
        out_specs=pl.BlockSpec((tm, d), lambda i: (i, 0)),
        out_shape=jax.ShapeDtypeStruct((n, d), BF16),
        compiler_params=_params("parallel"),
        name="rmsnorm_cast",
    )(x, g.reshape(1, d))


def _matmul_kernel(x_ref, w_ref, o_ref):
    o_ref[...] = jnp.dot(x_ref[...], w_ref[...], preferred_element_type=F32).astype(o_ref.dtype)


def _matmul_cols(x, w, col_block, ncols, out_dtype):
    n, k = x.shape
    tm = _row_block(n, 512, 16)
    return pl.pallas_call(
        _matmul_kernel,
        grid=(n // tm,),
        in_specs=[
            pl.BlockSpec((tm, k), lambda i: (i, 0)),
            pl.BlockSpec((k, ncols), lambda i: (0, col_block), pipeline_mode=pl.Buffered(1)),
        ],
        out_specs=pl.BlockSpec((tm, ncols), lambda i: (i, 0)),
        out_shape=jax.ShapeDtypeStruct((n, ncols), out_dtype),
        compiler_params=_params("parallel"),
        name="in_proj",
    )(x, w)


def _sb_kernel(*refs, n_new, n_cache, q_rows, chains, joint, scale):
    q_ref = refs[0]
    o_ref, kb_ref, vb_ref = refs[-3:]
    first_block = SB_LOOP_BLOCKS
    head0 = first_block * LANES + LANES - N_META
    new0 = head0 + n_cache
    n_scratch = kb_ref.shape[0]

    def fill(src_refs, n_rows, dst0):
        def chunk(c, carry):
            src = pl.multiple_of(c * LANES, LANES)
            dst = pl.multiple_of(dst0 + c * LANES, 16)
            for src_ref, dst_ref in zip(src_refs, (kb_ref, vb_ref)):
                dst_ref[pl.ds(dst, LANES), :] = src_ref[pl.ds(src, LANES), :].astype(BF16)
            return carry

        full = n_rows // LANES
        if full:
            lax.fori_loop(0, full, chunk, 0)
        if n_rows % LANES:
            for src_ref, dst_ref in zip(src_refs, (kb_ref, vb_ref)):
                dst_ref[dst0 + full * LANES:dst0 + n_rows, :] = src_ref[full * LANES:n_rows, :].astype(BF16)

    zeros_head = jnp.zeros((head0, LANES), BF16)
    kb_ref[0:head0, :] = zeros_head
    vb_ref[0:head0, :] = zeros_head
    if joint:
        fill(refs[1:3], n_cache + n_new, head0)
    else:
        fill(refs[3:5], n_cache, head0)
        fill(refs[1:3], n_new, new0)
    if new0 + n_new < n_scratch:
        tail = jnp.zeros((n_scratch - new0 - n_new, LANES), BF16)
        kb_ref[new0 + n_new:n_scratch, :] = tail
        vb_ref[new0 + n_new:n_scratch, :] = tail

    row = lax.broadcasted_iota(jnp.int32, (q_rows, LANES), 0)
    col = lax.broadcasted_iota(jnp.int32, (q_rows, LANES), 1)

    def later_matrix(width):
        kr = lax.broadcasted_iota(jnp.int32, (width, width), 0)
        kc = lax.broadcasted_iota(jnp.int32, (width, width), 1)
        return jnp.concatenate([jnp.where(kr > kc, 1.0, 0.0), jnp.ones((width, LANES), F32)], axis=1).astype(BF16)

    later = {n: later_matrix(n * LANES) for n in {SB_FIRST_BLOCKS, SB_LOOP_BLOCKS}}

    def segments(qbs, blocks, n_blocks, biases, carries, accs):
        width = n_blocks * LANES
        zs, sps, logs = [], [], []
        for qb, blk, bias in zip(qbs, blocks, biases):
            s0 = pl.multiple_of(blk * LANES, LANES)
            z = lax.dot_general(qb, kb_ref[pl.ds(s0, width), :], _NT, preferred_element_type=F32) * scale + bias
            sp = jnp.log(1.0 + jnp.exp(-jnp.abs(z)))
            zs.append(z)
            sps.append(sp)
            logs.append(-(jnp.maximum(z, 0.0) + sp))
        hi, lo = _split_bf16(jnp.concatenate(logs, axis=0))
        sums = jnp.dot(jnp.concatenate([hi, lo], axis=0), later[n_blocks], preferred_element_type=F32)
        sums = sums[:chains * q_rows] + sums[chains * q_rows:]
        new_carries, new_accs = [], []
        for c in range(chains):
            s = sums[c * q_rows:(c + 1) * q_rows]
            w = jnp.exp(jnp.minimum(zs[c], 0.0) - sps[c] + (s[:, :width] + jnp.tile(carries[c], (1, n_blocks))))
            v0 = pl.multiple_of(blocks[c] * LANES, LANES)
            new_accs.append(accs[c] + jnp.dot(w.astype(BF16), vb_ref[pl.ds(v0, width), :],
                                              preferred_element_type=F32))
            new_carries.append(carries[c] + s[:, width:])
        return tuple(new_carries), tuple(new_accs)

    causal_bias = jnp.where(col < row, 0.0, SB_MASKED)
    not_first_positions = jnp.where(col >= LANES - N_META, 0.0, 1.0)

    def block_bias(block):
        return (not_first_positions * jnp.where(block == first_block, SB_MASKED, 0.0)
                + jnp.where(block < first_block, SB_MASKED, 0.0))

    def query_group(g, carry_unused):
        q0s = [pl.multiple_of((g * chains + c) * q_rows, q_rows) for c in range(chains)]
        qbs = [q_ref[pl.ds(q0, q_rows), :] for q0 in q0s]
        diag = [(new0 + (g * chains + c) * q_rows) // LANES for c in range(chains)]
        zero = jnp.zeros((q_rows, LANES), F32)
        first = [diag[c] - (SB_FIRST_BLOCKS - 1) for c in range(chains)]
        biases = [jnp.concatenate([block_bias(first[c] + k) for k in range(SB_FIRST_BLOCKS - 1)] + [causal_bias],
                                  axis=1) for c in range(chains)]
        carries, accs = segments(qbs, first, SB_FIRST_BLOCKS, biases, (zero,) * chains, (zero,) * chains)

        def oldest(c, trip):
            return first[c] - SB_LOOP_BLOCKS * (trip + 1)

        def cond(state):
            trip, carries, _ = state
            go = jnp.bool_(False)
            for c in range(chains):
                alive = oldest(c, trip) + SB_LOOP_BLOCKS - 1 >= first_block
                go = jnp.logical_or(go, jnp.logical_and(alive, jnp.max(carries[c]) > SB_LOG_CUTOFF))
            return go

        def body(state):
            trip, carries, accs = state
            blocks = [jnp.maximum(oldest(c, trip), 0) for c in range(chains)]
            biases = [jnp.concatenate([block_bias(blocks[c] + k) for k in range(SB_LOOP_BLOCKS)], axis=1)
                      for c in range(chains)]
            carries, accs = segments(qbs, blocks, SB_LOOP_BLOCKS, biases, carries, accs)
            return trip + 1, carries, accs

        _, _, accs = lax.while_loop(cond, body, (jnp.int32(0), carries, accs))
        for c in range(chains):
            o_ref[pl.ds(q0s[c], q_rows), :] = accs[c].astype(o_ref.dtype)
        return carry_unused

    lax.fori_loop(0, n_new // (q_rows * chains), query_group, 0)


def _sb_attention(q, keys, values, *, n_streams, n_new, n_cache, heads):
    assert (n_cache - N_META) % LANES == 0
    joint = not isinstance(keys, tuple)
    q_rows = min(n_new, LANES)
    assert n_new % q_rows == 0 and q_rows % 16 == 0
    new0 = (SB_LOOP_BLOCKS + 1) * LANES + (n_cache - N_META)
    n_scratch = new0 + (n_new // q_rows - 1) * q_rows + LANES
    stream_head = lambda b, h: (b, h)
    new_spec = pl.BlockSpec((n_new, LANES), stream_head)
    if joint:
        kv_specs = [pl.BlockSpec((n_cache + n_new, LANES), stream_head)] * 2
        kv_args = (keys, values)
    else:
        kv_specs = [new_spec, new_spec] + [pl.BlockSpec((n_cache, LANES), stream_head)] * 2
        kv_args = (keys[0], values[0], keys[1], values[1])
    n_qblocks = n_new // q_rows
    chains = SB_CHAINS if n_qblocks % SB_CHAINS == 0 else 1
    kernel = functools.partial(_sb_kernel, n_new=n_new, n_cache=n_cache, q_rows=q_rows, chains=chains, joint=joint,
                               scale=1.0 / math.sqrt(LANES))
    return pl.pallas_call(
        kernel,
        grid=(n_streams, heads),
        in_specs=[new_spec] + kv_specs,
        out_specs=new_spec,
        out_shape=jax.ShapeDtypeStruct(q.shape, BF16),
        scratch_shapes=[pltpu.VMEM((n_scratch, LANES), BF16), pltpu.VMEM((n_scratch, LANES), BF16)],
        compiler_params=_params("parallel", "parallel"),
        name="sb_attention",
    )(q, *kv_args)


def _lru_kernel(xr_ref, yg_ref, buf_ref, h0_ref, cw_ref, cb_ref, wa_ref, ba_ref, wx_ref, bx_ref, lam_ref,
                m_ref, nbuf_ref, hl_ref, xp_ref, h_ref, *, n_rows, chunk):
    pad = SUBLANES
    n_hist = CONV_WIDTH - 1
    xp_ref[pad - n_hist:pad, :] = buf_ref[0]

    def copy(c, carry):
        r0 = pl.multiple_of(c * chunk, chunk)
        xp_ref[pl.ds(r0 + pad, chunk), :] = xr_ref[pl.ds(r0, chunk), :]
        return carry

    lax.fori_loop(0, n_rows // chunk, copy, 0)
    nbuf_ref[0] = xp_ref[n_rows + pad - n_hist:n_rows + pad, :]

    lam = lam_ref[...]
    sp_lam = jnp.maximum(-lam, 0.0) + _softplus_neg_abs(lam)
    cw = cw_ref[...]
    cb = cb_ref[...]
    wa = wa_ref[0]
    wx = wx_ref[0]
    ba = ba_ref[...]
    bx = bx_ref[...]
    groups = chunk // SUBLANES
    sub = lax.broadcasted_iota(jnp.int32, (groups, SUBLANES, LANES), 1)
    h_ref[...] = h0_ref[0]

    def step(c, carry):
        r0 = pl.multiple_of(c * chunk, chunk)
        xc = cb
        for tap in range(CONV_WIDTH):
            xc = xc + cw[tap:tap + 1, :] * xp_ref[pl.ds(r0 + pad - n_hist + tap, chunk), :]
        xcb = xc.astype(BF16)
        r = jax.nn.sigmoid(jnp.dot(xcb, wa, preferred_element_type=F32) + ba)
        gate_i = jax.nn.sigmoid(jnp.dot(xcb, wx, preferred_element_type=F32) + bx)
        log_a = -RG_C * r * sp_lam
        a = jnp.exp(log_a)
        t = jnp.tanh(log_a)
        u = jnp.sqrt(-2.0 * t / (1.0 - t)) * (gate_i * xc)
        a3 = a.reshape(groups, SUBLANES, LANES)
        u3 = u.reshape(groups, SUBLANES, LANES)
        shift = 1
        while shift < SUBLANES:
            keep = sub >= shift
            u3 = jnp.where(keep, a3 * pltpu.roll(u3, shift, axis=1) + u3, u3)
            a3 = jnp.where(keep, a3 * pltpu.roll(a3, shift, axis=1), a3)
            shift *= 2
        h = h_ref[...]
        rows = []
        for g in range(groups):
            hg = a3[g] * h + u3[g]
            h = hg[SUBLANES - 1:SUBLANES, :]
            rows.append(hg)
        h_ref[...] = h
        hs = jnp.concatenate(rows, axis=0)
        m_ref[pl.ds(r0, chunk), :] = (jax.nn.gelu(yg_ref[pl.ds(r0, chunk), :]) * hs).astype(m_ref.dtype)
        return carry

    n_chunks = n_rows // chunk
    lax.fori_loop(0, n_chunks, step, 0, unroll=2 if n_chunks % 2 == 0 else 1)
    hl_ref[0] = h_ref[...]


def _conv_rglru(xr, yg, buf, h0, conv_w, conv_b, w_a, b_a, w_x, b_x, lam, *, n_streams, n_rows, shared_state):
    width = xr.shape[1]
    heads = width // LANES
    chunk = _row_block(n_rows, LANES, 16)
    state_map = (lambda b, c: (0, 0, c)) if shared_state else (lambda b, c: (b, 0, c))
    seq_spec = pl.BlockSpec((n_rows, LANES), lambda b, c: (b, c))
    vec_spec = pl.BlockSpec((1, LANES), lambda b, c: (0, c))
    gate_spec = pl.BlockSpec((1, LANES, LANES), lambda b, c: (c, 0, 0))
    kernel = functools.partial(_lru_kernel, n_rows=n_rows, chunk=chunk)
    return pl.pallas_call(
        kernel,
        grid=(n_streams, heads),
        in_specs=[seq_spec, seq_spec,
                  pl.BlockSpec((1, CONV_WIDTH - 1, LANES), state_map), pl.BlockSpec((1, 1, LANES), state_map),
                  pl.BlockSpec((CONV_WIDTH, LANES), lambda b, c: (0, c)), vec_spec,
                  gate_spec, vec_spec, gate_spec, vec_spec, vec_spec],
        out_specs=[seq_spec,
                   pl.BlockSpec((1, CONV_WIDTH - 1, LANES), lambda b, c: (b, 0, c)),
                   pl.BlockSpec((1, 1, LANES), lambda b, c: (b, 0, c))],
        out_shape=[jax.ShapeDtypeStruct(xr.shape, BF16),
                   jax.ShapeDtypeStruct((n_streams, CONV_WIDTH - 1, width), F32),
                   jax.ShapeDtypeStruct((n_streams, 1, width), F32)],
        scratch_shapes=[pltpu.VMEM((n_rows + SUBLANES, LANES), F32), pltpu.VMEM((1, LANES), F32)],
        compiler_params=_params("parallel", "parallel"),
        name="conv_rglru",
    )(xr, yg, buf, h0, conv_w, conv_b.reshape(1, width), w_a, b_a.reshape(1, width), w_x, b_x.reshape(1, width),
      lam.reshape(1, width))


def _mix_kernel(attn_ref, m_ref, gs_ref, gl_ref, bgs_ref, bgl_ref, x_ref, wpa_ref, wpl_ref, wo_ref, nf_ref,
                x1_ref, xn_ref):
    y_sb = jnp.dot(attn_ref[...], wpa_ref[...], preferred_element_type=F32)
    y_lru = jnp.dot(m_ref[...], wpl_ref[...], preferred_element_type=F32)
    g_sb = jax.nn.sigmoid(gs_ref[...] + bgs_ref[...])
    g_lru = jax.nn.sigmoid(gl_ref[...] + bgl_ref[...])
    mix = (g_sb * y_sb + g_lru * y_lru).astype(BF16)
    x1 = x_ref[...] + jnp.dot(mix, wo_ref[...], preferred_element_type=F32)
    x1_ref[...] = x1
    ms = jnp.mean(x1 * x1, axis=-1, keepdims=True)
    xn_ref[...] = (x1 * lax.rsqrt(ms + RMS_EPS) * nf_ref[...]).astype(xn_ref.dtype)


def _mix_project(attn, m, gs, gl, b_gate, x, w_pa, w_pl, w_o, norm_ffn):
    n, d = x.shape
    tm = _row_block(n, 256, 16)
    row = lambda i: (i, 0)
    fixed = lambda i: (0, 0)
    blk = pl.BlockSpec((tm, d), row)
    vec = pl.BlockSpec((1, d), fixed)
    wspec = pl.BlockSpec((d, d), fixed, pipeline_mode=pl.Buffered(1))
    return pl.pallas_call(
        _mix_kernel,
        grid=(n // tm,),
        in_specs=[blk, blk, blk, blk, vec, vec, blk, wspec, wspec, wspec, vec],
        out_specs=[blk, blk],
        out_shape=[jax.ShapeDtypeStruct((n, d), F32), jax.ShapeDtypeStruct((n, d), BF16)],
        compiler_params=_params("parallel"),
        name="mix_project",
    )(attn, m, gs, gl, b_gate[:d].reshape(1, d), b_gate[d:].reshape(1, d), x, w_pa, w_pl, w_o,
      norm_ffn.reshape(1, d))


def _top_rows(s, k):
    n = s.shape[0]
    rows = lax.broadcasted_iota(jnp.int32, s.shape, 0)
    vals, idxs = [], []
    for _ in range(k):
        m = jnp.max(s, axis=0, keepdims=True)
        idx = jnp.min(jnp.where(s == m, rows, n), axis=0, keepdims=True)
        s = jnp.where(rows == idx, -jnp.inf, s)
        vals.append(m)
        idxs.append(idx)
        yield
    return jnp.concatenate(vals, axis=0), jnp.concatenate(idxs, axis=0)


def _finish(stages):
    while True:
        try:
            next(stages)
        except StopIteration as done:
            return done.value


def _take_rows(table, idx):
    out = jnp.zeros(idx.shape, table.dtype)
    for r in range(table.shape[0]):
        out = jnp.where(idx == r, table[r:r + 1, :], out)
    return out


def _route_unit(q1, q2, k1, k2):
    t1, i1 = yield from _top_rows(lax.dot_general(k1, q1, _NT, preferred_element_type=F32), PEER_TOPK)
    t2, i2 = yield from _top_rows(lax.dot_general(k2, q2, _NT, preferred_element_type=F32), PEER_TOPK)
    cand = jnp.concatenate(
        [t1[0:1, :] + t2]
        + [t1[a:a + 1, :] + t2[0:8, :] for a in range(1, 4)]
        + [t1[a:a + 1, :] + t2[0:4, :] for a in range(4, 8)]
        + [t1[8:16, :] + t2[0:1, :]], axis=0)
    score, r = yield from _top_rows(cand, PEER_TOPK)
    rank1 =jnp.where(r < 16, 0, jnp.where(r < 40, 1 + lax.shift_right_logical(r - 16, 3),
                                           jnp.where(r < 56, 4 + lax.shift_right_logical(r - 40, 2), r - 48)))
    rank2 = jnp.where(r < 16, r, jnp.where(r < 40, jnp.bitwise_and(r - 16, 7),
                                           jnp.where(r < 56, jnp.bitwise_and(r - 40, 3), 0)))
    e1 = _take_rows(i1, rank1)
    e2 = _take_rows(i2, rank2)
    ex = jnp.exp(score - jnp.max(score, axis=0, keepdims=True))
    gate = ex / jnp.sum(ex, axis=0, keepdims=True)
    return e1.astype(F32), e2.astype(F32), gate


def _query_kernel(x_ref, w_ref, o_ref):
    q = jnp.dot(x_ref[...], w_ref[...], preferred_element_type=F32).astype(o_ref.dtype)
    for part in range(o_ref.shape[0]):
        o_ref[part] = q[:, part * LANES:(part + 1) * LANES]


def _query_proj(xn, wq):
    n, d = xn.shape
    parts = wq.shape[1] // LANES
    tm = _row_block(n, 512, 16)
    return pl.pallas_call(
        _query_kernel,
        grid=(n // tm,),
        in_specs=[pl.BlockSpec((tm, d), lambda i: (i, 0)),
                  pl.BlockSpec(wq.shape, lambda i: (0, 0), pipeline_mode=pl.Buffered(1))],
        out_specs=pl.BlockSpec((parts, tm, LANES), lambda i: (0, i, 0)),
        out_shape=jax.ShapeDtypeStruct((parts, n, LANES), BF16),
        compiler_params=_params("parallel"),
        name="query_proj",
    )(xn, wq)


def _route_kernel(q_ref, k1_ref, k2_ref, e1_ref, e2_ref, g_ref, e1s_ref, e2s_ref, gs_ref):
    k1 = k1_ref[...]
    k2 = k2_ref[...]

    def head(h, carry):
        e1, e2, gate = _finish(_route_unit(q_ref[2 * h], q_ref[2 * h + 1], k1, k2))
        out0 = pl.multiple_of(h * PEER_TOPK, PEER_TOPK)
        e1s_ref[pl.ds(out0, PEER_TOPK), :] = e1
        e2s_ref[pl.ds(out0, PEER_TOPK), :] = e2
        gs_ref[pl.ds(out0, PEER_TOPK), :] = gate
        return carry

    lax.fori_loop(0, PEER_HEADS, head, 0)
    e1_ref[...] = e1s_ref[...].T
    e2_ref[...] = e2s_ref[...].T
    g_ref[...] = gs_ref[...].T


def _peer_route_first(q3, k1, k2, n_first):
    parts = q3.shape[0]
    slots = PEER_HEADS * PEER_TOPK
    fixed = lambda i: (0, 0)
    out = pl.BlockSpec((LANES, slots), lambda i: (i, 0))
    return pl.pallas_call(
        _route_kernel,
        grid=(n_first // LANES,),
        in_specs=[pl.BlockSpec((parts, LANES, LANES), lambda i: (0, i, 0)),
                  pl.BlockSpec(k1.shape, fixed), pl.BlockSpec(k2.shape, fixed)],
        out_specs=[out, out, out],
        out_shape=[jax.ShapeDtypeStruct((n_first, slots), F32)] * 3,
        scratch_shapes=[pltpu.VMEM((slots, LANES), F32)] * 3,
        compiler_params=_params("parallel"),
        name="peer_route_first",
    )(q3, k1, k2)


PEER_CHUNK_KEYS = 2


def _gate_tile_bits(n, keys, e1_ref, e2_ref, g_ref):
    first = jnp.where(keys == e1_ref[pl.ds(n, 1), :], g_ref[pl.ds(n, 1), :], 0.0).astype(BF16)
    second = jnp.where(keys == e2_ref[pl.ds(n, 1), :], 1.0, 0.0).astype(BF16)
    c = lax.dot_general(first, second, _NT, preferred_element_type=F32)
    return lax.bitcast_convert_type(c.astype(BF16).astype(F32), jnp.uint32)


def _peer_kernel(xn_ref, qn_ref, e1f_ref, e2f_ref, gf_ref, k1_ref, k2_ref, u0_ref, ua_ref, ub_ref, v_ref, x1_ref,
                 nf_ref, y_ref, c_ref, hid_a_ref, hid_b_ref, e1_ref, e2_ref, g_ref, e1n_ref, e2n_ref, gn_ref, *, tn):
    i = pl.program_id(0)
    j = pl.program_id(1)
    n_steps = pl.num_programs(1)
    half = tn // 2
    lane_blocks = tn // LANES
    ck = PEER_CHUNK_KEYS
    high = jnp.uint32(0xFFFF0000)

    @pl.when(j == 0)
    def _():
        @pl.when(i == 0)
        def _():
            e1_ref[...] = e1f_ref[...]
            e2_ref[...] = e2f_ref[...]
            g_ref[...] = gf_ref[...]

        y_ref[...] = jnp.zeros_like(y_ref)
        keys = lax.broadcasted_iota(jnp.int32, (N_KEYS, LANES), 0).astype(F32)

        def build(p, carry):
            bits = _gate_tile_bits(p, keys, e1_ref, e2_ref, g_ref)
            bits = bits | (_gate_tile_bits(p + half, keys, e1_ref, e2_ref, g_ref) >> 16)
            c_ref[pl.ds(pl.multiple_of(p * C_PITCH, 4), N_KEYS), :] = bits
            return carry

        lax.fori_loop(0, half, build, 0, unroll=32)
        hid_a_ref[...] = lax.dot_general(xn_ref[...], u0_ref[...], _NT, preferred_element_type=F32)

    def weighted(hid_ref, first_key):
        parts = []
        for t in range(ck):
            bits = c_ref[pl.ds(first_key + t, half, stride=C_PITCH), :]
            c = jnp.concatenate([lax.bitcast_convert_type(bits & high, F32),
                                 lax.bitcast_convert_type(bits << 16, F32)], axis=0)
            parts.append((c * jax.nn.gelu(hid_ref[:, t * N_KEYS:(t + 1) * N_KEYS])).astype(BF16))
        return parts

    head = (j // lane_blocks) % PEER_HEADS
    lane_block = j % lane_blocks
    tok0 = pl.multiple_of(lane_block * LANES, LANES)
    xn = xn_ref[...]
    rows = ck * N_KEYS
    routing = _route_unit(qn_ref[2 * head, pl.ds(tok0, LANES), :], qn_ref[2 * head + 1, pl.ds(tok0, LANES), :],
                          k1_ref[...], k2_ref[...])

    def route_some(count):
        for _ in range(count):
            next(routing, None)

    def add_weighted(w, v_rows, extractions):
        for col in range(0, y_ref.shape[1], MXU_COLS):
            y_ref[:, col:col + MXU_COLS] += jnp.dot(w, v_ref[v_rows:v_rows + rows, col:col + MXU_COLS],
                                                    preferred_element_type=F32)
            route_some(extractions)

    w_a = jnp.concatenate(weighted(hid_a_ref, 2 * ck * j), axis=1)
    hid_b_ref[...] = lax.dot_general(xn, ua_ref[...], _NT, preferred_element_type=F32)
    route_some(4)
    add_weighted(w_a, 0, 2)
    w_b = jnp.concatenate(weighted(hid_b_ref, 2 * ck * j + ck), axis=1)
    hid_a_ref[...] = lax.dot_general(xn, ub_ref[...], _NT, preferred_element_type=F32)
    route_some(4)
    add_weighted(w_b, rows, 3)
    e1n, e2n, gn = _finish(routing)

    slot0 = pl.multiple_of(head * PEER_TOPK, PEER_TOPK)
    e1n_ref[lane_block, pl.ds(slot0, PEER_TOPK), :] = e1n
    e2n_ref[lane_block, pl.ds(slot0, PEER_TOPK), :] = e2n
    gn_ref[lane_block, pl.ds(slot0, PEER_TOPK), :] = gn

    @pl.when(j == n_steps - 1)
    def _():
        x2 = x1_ref[...] + y_ref[...]
        ms = jnp.mean(x2 * x2, axis=-1, keepdims=True)
        y_ref[...] = x2 * lax.rsqrt(ms + RMS_EPS) * nf_ref[...]
        for lb in range(lane_blocks):
            e1_ref[lb * LANES:(lb + 1) * LANES, :] = e1n_ref[lb].T
            e2_ref[lb * LANES:(lb + 1) * LANES, :] = e2n_ref[lb].T
            g_ref[lb * LANES:(lb + 1) * LANES, :] = gn_ref[lb].T


def _peer_experts(xn, q3, e1_first, e2_first, g_first, k1, k2, u, v, x1, norm_final):
    n, d = xn.shape
    tn = e1_first.shape[0]
    rows = PEER_CHUNK_KEYS * N_KEYS
    n_chunks = u.shape[0] // rows
    n_blocks = n // tn
    slots = PEER_HEADS * PEER_TOPK
    assert n_chunks % 2 == 0 and n % tn == 0 and tn % LANES == 0
    assert n_chunks // 2 >= PEER_HEADS * (tn // LANES), "one routing unit per grid step must cover the next block"
    blk = lambda i, j: (i, 0)
    fixed = lambda i, j: (0, 0)
    first = pl.BlockSpec((tn, slots), fixed)
    kernel = functools.partial(_peer_kernel, tn=tn)
    return pl.pallas_call(
        kernel,
        grid=(n_blocks, n_chunks // 2),
        in_specs=[pl.BlockSpec((tn, d), blk, pipeline_mode=pl.Buffered(1)),
                  pl.BlockSpec((q3.shape[0], tn, LANES), lambda i, j: (0, jnp.minimum(i + 1, n_blocks - 1), 0)),
                  first, first, first,
                  pl.BlockSpec(k1.shape, fixed), pl.BlockSpec(k2.shape, fixed),
                  pl.BlockSpec((rows, d), fixed, pipeline_mode=pl.Buffered(1)),
                  pl.BlockSpec((rows, d), lambda i, j: (2 * j + 1, 0)),
                  pl.BlockSpec((rows, d), lambda i, j: (jnp.minimum(2 * j + 2, n_chunks - 1), 0)),
                  pl.BlockSpec((2 * rows, d), lambda i, j: (j, 0)),
                  pl.BlockSpec((tn, d), blk, pipeline_mode=pl.Buffered(1)),
                  pl.BlockSpec((1, d), fixed)],
        out_specs=pl.BlockSpec((tn, d), blk),
        out_shape=jax.ShapeDtypeStruct((n, d), F32),
        scratch_shapes=[pltpu.VMEM((tn // 2 * C_PITCH, LANES), jnp.uint32),
                        pltpu.VMEM((tn, rows), F32), pltpu.VMEM((tn, rows), F32)]
                       + [pltpu.VMEM((tn, slots), F32)] * 3
                       + [pltpu.VMEM((tn // LANES, slots, LANES), F32)] * 3,
        compiler_params=_params("arbitrary", "arbitrary"),
        name="peer_experts",
    )(xn, q3, e1_first, e2_first, g_first, k1, k2, u, u, u, v, x1, norm_final.reshape(1, d))


def kernel(x_prompt, x_sample, cache_sb_k, cache_sb_v, state_conv, state_lru, meta_tokens, norm_mix, norm_ffn, w_in, b_gate, conv_w, conv_b, w_rg_a, b_rg_a, w_rg_x, b_rg_x, lru_lambda, w_proj_attn, w_proj_lru, w_out, w_query, sub_keys, expert_u, expert_v, norm_final):
    depth = w_in.shape[0]
    assert depth == 1, "one layer: the meta tokens' residual stream is never read after the mixer state"
    batch, seq, d = x_prompt.shape
    dec_batch, dec_seq, _ = x_sample.shape
    n_cache = cache_sb_k.shape[2]
    heads = cache_sb_k.shape[3]
    n_frames = batch * seq
    n_sample = dec_batch * dec_seq
    n_meta = meta_tokens.shape[0]
    assert n_meta == N_META and d == heads * LANES

    l = 0
    w_in_b = w_in[l].astype(BF16)
    w_a = w_rg_a[l].astype(BF16)
    w_x = w_rg_x[l].astype(BF16)
    w_pa = w_proj_attn[l].astype(BF16)
    w_pl = w_proj_lru[l].astype(BF16)
    w_o = w_out[l].astype(BF16)
    wq_b = w_query[l].astype(BF16)
    k1 = sub_keys[l, 0].astype(BF16)
    k2 = sub_keys[l, 1].astype(BF16)
    u_b = expert_u[l].astype(BF16)
    v_b = expert_v[l].astype(BF16)

    x_f = x_prompt.reshape(n_frames, d)
    x_s = jnp.concatenate([x_sample.reshape(n_sample, d), meta_tokens.astype(x_sample.dtype)], axis=0)

    col_q, col_k, col_v = 0, 1, 2
    other_cols = (3, 4, 5, 6)

    def project(xn, cols, dtype):
        return [_matmul_cols(xn, w_in_b, g, d, dtype) for g in cols]

    xn_f = _rmsnorm_cast(x_f, norm_mix[l])
    xn_s = _rmsnorm_cast(x_s, norm_mix[l])
    (q_f,), (q_s,) = project(xn_f, (col_q,), BF16), project(xn_s, (col_q,), BF16)
    xr_f, yg_f, gs_f, gl_f = project(xn_f, other_cols, F32)
    k_s, v_s, xr_s, yg_s, gs_s, gl_s = project(xn_s, (col_k, col_v) + other_cols, F32)
    xn_meta = jnp.broadcast_to(xn_s[n_sample:][None], (batch, n_meta, d))
    xn_p = jnp.concatenate([xn_meta, xn_f.reshape(batch, seq, d)], axis=1).reshape(batch * (n_meta + seq), d)
    k_p, v_p = project(xn_p, (col_k, col_v), F32)

    lru_w = (conv_w[l], conv_b[l], w_a, b_rg_a[l], w_x, b_rg_x[l], lru_lambda[l])
    _, buf_meta, h_meta = _conv_rglru(
        xr_s[n_sample:], yg_s[n_sample:], jnp.zeros((1, CONV_WIDTH - 1, d), F32), jnp.zeros((1, 1, d), F32),
        *lru_w, n_streams=1, n_rows=n_meta, shared_state=True)
    m_f, buf_f, h_f = _conv_rglru(xr_f, yg_f, buf_meta, h_meta, *lru_w,
                                  n_streams=batch, n_rows=seq, shared_state=True)
    m_s, buf_s, h_s = _conv_rglru(xr_s[:n_sample], yg_s[:n_sample], state_conv[l], state_lru[l][:, None, :],
                                  *lru_w, n_streams=dec_batch, n_rows=dec_seq, shared_state=False)

    attn_f = _sb_attention(q_f, k_p, v_p, n_streams=batch, n_new=seq, n_cache=n_meta, heads=heads)
    attn_s = _sb_attention(q_s[:n_sample],
                           (k_s[:n_sample], cache_sb_k[l].reshape(dec_batch * n_cache, d)),
                           (v_s[:n_sample], cache_sb_v[l].reshape(dec_batch * n_cache, d)),
                           n_streams=dec_batch, n_new=dec_seq, n_cache=n_cache, heads=heads)

    def finish(attn, m, gs, gl, x):
        x1, xn2 = _mix_project(attn, m, gs, gl, b_gate[l], x, w_pa, w_pl, w_o, norm_ffn[l])
        q3 = _query_proj(xn2, wq_b)
        tn = _row_block(xn2.shape[0], 512, LANES)
        e1, e2, gate = _peer_route_first(q3, k1, k2, tn)
        return _peer_experts(xn2, q3, e1, e2, gate, k1, k2, u_b, v_b, x1, norm_final)

    y_f = finish(attn_f, m_f, gs_f, gl_f, x_f)
    y_s = finish(attn_s, m_s[:n_sample], gs_s[:n_sample], gl_s[:n_sample], x_s[:n_sample])

    return (y_f.reshape(batch, seq, d), y_s.reshape(dec_batch, dec_seq, d),
            k_p.reshape(1, batch, n_meta + seq, heads, LANES), v_p.reshape(1, batch, n_meta + seq, heads, LANES),
            buf_f[None], h_f.reshape(1, batch, d),
            k_s[:n_sample].reshape(1, dec_batch, dec_seq, heads, LANES),
            v_s[:n_sample].reshape(1, dec_batch, dec_seq, heads, LANES),
            buf_s[None], h_s.reshape(1, dec_batch, d))
```

```python
import functools
import math

import jax
import jax.numpy as jnp
from jax import lax
from jax.experimental import pallas as pl
from jax.experimental.pallas import tpu as pltpu

F32 = jnp.float32
BF16 = jnp.bfloat16

LANES = 128
SUBLANES = 8
VMEM_LIMIT_BYTES = 56 * 1024 * 1024
MXU_COLS = 256

RMS_EPS = 1e-6
RG_C = 8.0
CONV_WIDTH = 4
N_META = 16
PEER_TOPK = 16
PEER_HEADS = 8
N_KEYS = 128
SB_LOG_CUTOFF = -105.0
SB_MASKED = -1e30
SB_CHAINS = 8
SB_FIRST_BLOCKS = 3
SB_LOOP_BLOCKS = 2
C_PITCH = 132

_NT = (((1,), (1,)), ((), ()))


def _params(*sem):
    return pltpu.CompilerParams(dimension_semantics=sem, vmem_limit_bytes=VMEM_LIMIT_BYTES)


def _row_block(n, target, align):
    best = None
    for t in range(align, min(n, target) + 1, align):
        if n % t == 0:
            best = t
    assert best is not None, (n, target, align)
    return best


def _softplus_neg_abs(z):
    return jnp.log1p(jnp.exp(-jnp.abs(z)))


def _split_bf16(x):
    hi = x.astype(BF16)
    lo = (x - hi.astype(F32)).astype(BF16)
    return hi, lo


def _norm_project_kernel(x_ref, g_ref, w_ref, xn_ref, o_ref):
    x = x_ref[...]
    ms = jnp.mean(x * x, axis=-1, keepdims=True)
    xn = (x * lax.rsqrt(ms + RMS_EPS) * g_ref[...]).astype(xn_ref.dtype)
    xn_ref[...] = xn
    o_ref[...] = jnp.dot(xn, w_ref[...], preferred_element_type=F32).astype(o_ref.dtype)


def _norm_project(x, g, w, col_block, ncols, out_dtype):
    n, d = x.shape
    tm = _row_block(n, 512, 16)
    rows = lambda i: (i, 0)
    return pl.pallas_call(
        _norm_project_kernel,
        grid=(n // tm,),
        in_specs=[pl.BlockSpec((tm, d), rows), pl.BlockSpec((1, d), lambda i: (0, 0)),
                  pl.BlockSpec((d, ncols), lambda i: (0, col_block), pipeline_mode=pl.Buffered(1))],
        out_specs=[pl.BlockSpec((tm, d), rows), pl.BlockSpec((tm, ncols), rows)],
        out_shape=[jax.ShapeDtypeStruct((n, d), BF16), jax.ShapeDtypeStruct((n, ncols), out_dtype)],
        compiler_params=_params("parallel"),
        name="norm_project",
    )(x, g.reshape(1, d), w)


def _matmul_kernel(x_ref, w_ref, o_ref):
    o_ref[...] = jnp.dot(x_ref[...], w_ref[...], preferred_element_type=F32).astype(o_ref.dtype)


def _matmul_cols(x, w, col_block, ncols, out_dtype):
    n, k = x.shape
    tm = _row_block(n, 512, 16)
    return pl.pallas_call(
        _matmul_kernel,
        grid=(n // tm,),
        in_specs=[
            pl.BlockSpec((tm, k), lambda i: (i, 0)),
            pl.BlockSpec((k, ncols), lambda i: (0, col_block), pipeline_mode=pl.Buffered(1)),
        ],
        out_specs=pl.BlockSpec((tm, ncols), lambda i: (i, 0)),
        out_shape=jax.ShapeDtypeStruct((n, ncols), out_dtype),
        compiler_params=_params("parallel"),
        name="in_proj",
    )(x, w)


def _sb_kernel(*refs, n_new, n_cache, q_rows, chains, joint, scale):
    q_ref = refs[0]
    o_ref, kb_ref, vb_ref = refs[-3:]
    first_block = SB_LOOP_BLOCKS
    head0 = first_block * LANES + LANES - N_META
    new0 = head0 + n_cache
    n_scratch = kb_ref.shape[0]

    def fill(src_refs, n_rows, dst0):
        def chunk(c, carry):
            src = pl.multiple_of(c * LANES, LANES)
            dst = pl.multiple_of(dst0 + c * LANES, 16)
            for src_ref, dst_ref in zip(src_refs, (kb_ref, vb_ref)):
                dst_ref[pl.ds(dst, LANES), :] = src_ref[pl.ds(src, LANES), :].astype(BF16)
            return carry

        full = n_rows // LANES
        if full:
            lax.fori_loop(0, full, chunk, 0)
        if n_rows % LANES:
            for src_ref, dst_ref in zip(src_refs, (kb_ref, vb_ref)):
                dst_ref[dst0 + full * LANES:dst0 + n_rows, :] = src_ref[full * LANES:n_rows, :].astype(BF16)

    zeros_head = jnp.zeros((head0, LANES), BF16)
    kb_ref[0:head0, :] = zeros_head
    vb_ref[0:head0, :] = zeros_head
    if joint:
        fill(refs[1:3], n_cache + n_new, head0)
    else:
        fill(refs[3:5], n_cache, head0)
        fill(refs[1:3], n_new, new0)
    if new0 + n_new < n_scratch:
        tail = jnp.zeros((n_scratch - new0 - n_new, LANES), BF16)
        kb_ref[new0 + n_new:n_scratch, :] = tail
        vb_ref[new0 + n_new:n_scratch, :] = tail

    row = lax.broadcasted_iota(jnp.int32, (q_rows, LANES), 0)
    col = lax.broadcasted_iota(jnp.int32, (q_rows, LANES), 1)

    def later_matrix(width):
        kr = lax.broadcasted_iota(jnp.int32, (width, width), 0)
        kc = lax.broadcasted_iota(jnp.int32, (width, width), 1)
        return jnp.concatenate([jnp.where(kr > kc, 1.0, 0.0), jnp.ones((width, LANES), F32)], axis=1).astype(BF16)

    later = {n: later_matrix(n * LANES) for n in {SB_FIRST_BLOCKS, SB_LOOP_BLOCKS}}

    def segments(qbs, blocks, n_blocks, biases, carries, accs):
        width = n_blocks * LANES
        zs, sps, logs = [], [], []
        for qb, blk, bias in zip(qbs, blocks, biases):
            s0 = pl.multiple_of(blk * LANES, LANES)
            z = lax.dot_general(qb, kb_ref[pl.ds(s0, width), :], _NT, preferred_element_type=F32) * scale + bias
            sp = jnp.log(1.0 + jnp.exp(-jnp.abs(z)))
            zs.append(z)
            sps.append(sp)
            logs.append(-(jnp.maximum(z, 0.0) + sp))
        hi, lo = _split_bf16(jnp.concatenate(logs, axis=0))
        sums = jnp.dot(jnp.concatenate([hi, lo], axis=0), later[n_blocks], preferred_element_type=F32)
        sums = sums[:chains * q_rows] + sums[chains * q_rows:]
        new_carries, new_accs = [], []
        for c in range(chains):
            s = sums[c * q_rows:(c + 1) * q_rows]
            w = jnp.exp(jnp.minimum(zs[c], 0.0) - sps[c] + (s[:, :width] + jnp.tile(carries[c], (1, n_blocks))))
            v0 = pl.multiple_of(blocks[c] * LANES, LANES)
            new_accs.append(accs[c] + jnp.dot(w.astype(BF16), vb_ref[pl.ds(v0, width), :],
                                              preferred_element_type=F32))
            new_carries.append(carries[c] + s[:, width:])
        return tuple(new_carries), tuple(new_accs)

    causal_bias = jnp.where(col < row, 0.0, SB_MASKED)
    not_first_positions = jnp.where(col >= LANES - N_META, 0.0, 1.0)

    def block_bias(block):
        return (not_first_positions * jnp.where(block == first_block, SB_MASKED, 0.0)
                + jnp.where(block < first_block, SB_MASKED, 0.0))

    def query_group(g, carry_unused):
        q0s = [pl.multiple_of((g * chains + c) * q_rows, q_rows) for c in range(chains)]
        qbs = [q_ref[pl.ds(q0, q_rows), :] for q0 in q0s]
        diag = [(new0 + (g * chains + c) * q_rows) // LANES for c in range(chains)]
        zero = jnp.zeros((q_rows, LANES), F32)
        first = [diag[c] - (SB_FIRST_BLOCKS - 1) for c in range(chains)]
        biases = [jnp.concatenate([block_bias(first[c] + k) for k in range(SB_FIRST_BLOCKS - 1)] + [causal_bias],
                                  axis=1) for c in range(chains)]
        carries, accs = segments(qbs, first, SB_FIRST_BLOCKS, biases, (zero,) * chains, (zero,) * chains)

        def oldest(c, trip):
            return first[c] - SB_LOOP_BLOCKS * (trip + 1)

        def cond(state):
            trip, carries, _ = state
            go = jnp.bool_(False)
            for c in range(chains):
                alive = oldest(c, trip) + SB_LOOP_BLOCKS - 1 >= first_block
                go = jnp.logical_or(go, jnp.logical_and(alive, jnp.max(carries[c]) > SB_LOG_CUTOFF))
            return go

        def body(state):
            trip, carries, accs = state
            blocks = [jnp.maximum(oldest(c, trip), 0) for c in range(chains)]
            biases = [jnp.concatenate([block_bias(blocks[c] + k) for k in range(SB_LOOP_BLOCKS)], axis=1)
                      for c in range(chains)]
            carries, accs = segments(qbs, blocks, SB_LOOP_BLOCKS, biases, carries, accs)
            return trip + 1, carries, accs

        _, _, accs = lax.while_loop(cond, body, (jnp.int32(0), carries, accs))
        for c in range(chains):
            o_ref[pl.ds(q0s[c], q_rows), :] = accs[c].astype(o_ref.dtype)
        return carry_unused

    lax.fori_loop(0, n_new // (q_rows * chains), query_group, 0)


def _sb_attention(q, keys, values, *, n_streams, n_new, n_cache, heads):
    assert (n_cache - N_META) % LANES == 0
    joint = not isinstance(keys, tuple)
    q_rows = min(n_new, LANES)
    assert n_new % q_rows == 0 and q_rows % 16 == 0
    new0 = (SB_LOOP_BLOCKS + 1) * LANES + (n_cache - N_META)
    n_scratch = new0 + (n_new // q_rows - 1) * q_rows + LANES
    stream_head = lambda b, h: (b, h)
    new_spec = pl.BlockSpec((n_new, LANES), stream_head)
    if joint:
        kv_specs = [pl.BlockSpec((n_cache + n_new, LANES), stream_head)] * 2
        kv_args = (keys, values)
    else:
        kv_specs = [new_spec, new_spec] + [pl.BlockSpec((n_cache, LANES), stream_head)] * 2
        kv_args = (keys[0], values[0], keys[1], values[1])
    n_qblocks = n_new // q_rows
    chains = SB_CHAINS if n_qblocks % SB_CHAINS == 0 else 1
    kernel = functools.partial(_sb_kernel, n_new=n_new, n_cache=n_cache, q_rows=q_rows, chains=chains, joint=joint,
                               scale=1.0 / math.sqrt(LANES))
    return pl.pallas_call(
        kernel,
        grid=(n_streams, heads),
        in_specs=[new_spec] + kv_specs,
        out_specs=new_spec,
        out_shape=jax.ShapeDtypeStruct(q.shape, BF16),
        scratch_shapes=[pltpu.VMEM((n_scratch, LANES), BF16), pltpu.VMEM((n_scratch, LANES), BF16)],
        compiler_params=_params("parallel", "parallel"),
        name="sb_attention",
    )(q, *kv_args)


def _lru_kernel(xr_ref, yg_ref, buf_ref, h0_ref, cw_ref, cb_ref, wa_ref, ba_ref, wx_ref, bx_ref, lam_ref,
                m_ref, nbuf_ref, hl_ref, xp_ref, h_ref, *, n_rows, chunk):
    pad = SUBLANES
    n_hist = CONV_WIDTH - 1
    xp_ref[pad - n_hist:pad, :] = buf_ref[0]

    def copy(c, carry):
        r0 = pl.multiple_of(c * chunk, chunk)
        xp_ref[pl.ds(r0 + pad, chunk), :] = xr_ref[pl.ds(r0, chunk), :]
        return carry

    lax.fori_loop(0, n_rows // chunk, copy, 0)
    nbuf_ref[0] = xp_ref[n_rows + pad - n_hist:n_rows + pad, :]

    lam = lam_ref[...]
    sp_lam = jnp.maximum(-lam, 0.0) + _softplus_neg_abs(lam)
    cw = cw_ref[...]
    cb = cb_ref[...]
    wa = wa_ref[0]
    wx = wx_ref[0]
    ba = ba_ref[...]
    bx = bx_ref[...]
    groups = chunk // SUBLANES
    sub = lax.broadcasted_iota(jnp.int32, (groups, SUBLANES, LANES), 1)
    h_ref[...] = h0_ref[0]

    def step(c, carry):
        r0 = pl.multiple_of(c * chunk, chunk)
        xc = cb
        for tap in range(CONV_WIDTH):
            xc = xc + cw[tap:tap + 1, :] * xp_ref[pl.ds(r0 + pad - n_hist + tap, chunk), :]
        xcb = xc.astype(BF16)
        r = jax.nn.sigmoid(jnp.dot(xcb, wa, preferred_element_type=F32) + ba)
        gate_i = jax.nn.sigmoid(jnp.dot(xcb, wx, preferred_element_type=F32) + bx)
        log_a = -RG_C * r * sp_lam
        a = jnp.exp(log_a)
        t = jnp.tanh(log_a)
        u = jnp.sqrt(-2.0 * t / (1.0 - t)) * (gate_i * xc)
        a3 = a.reshape(groups, SUBLANES, LANES)
        u3 = u.reshape(groups, SUBLANES, LANES)
        shift = 1
        while shift < SUBLANES:
            keep = sub >= shift
            u3 = jnp.where(keep, a3 * pltpu.roll(u3, shift, axis=1) + u3, u3)
            a3 = jnp.where(keep, a3 * pltpu.roll(a3, shift, axis=1), a3)
            shift *= 2
        h = h_ref[...]
        rows = []
        for g in range(groups):
            hg = a3[g] * h + u3[g]
            h = hg[SUBLANES - 1:SUBLANES, :]
            rows.append(hg)
        h_ref[...] = h
        hs = jnp.concatenate(rows, axis=0)
        m_ref[pl.ds(r0, chunk), :] = (jax.nn.gelu(yg_ref[pl.ds(r0, chunk), :]) * hs).astype(m_ref.dtype)
        return carry

    n_chunks = n_rows // chunk
    lax.fori_loop(0, n_chunks, step, 0, unroll=8 if n_chunks % 8 == 0 else 1)
    hl_ref[0] = h_ref[...]


def _conv_rglru(xr, yg, buf, h0, conv_w, conv_b, w_a, b_a, w_x, b_x, lam, *, n_streams, n_rows, shared_state):
    width = xr.shape[1]
    heads = width // LANES
    chunk = _row_block(n_rows, LANES, 16)
    state_map = (lambda b, c: (0, 0, c)) if shared_state else (lambda b, c: (b, 0, c))
    seq_spec = pl.BlockSpec((n_rows, LANES), lambda b, c: (b, c))
    vec_spec = pl.BlockSpec((1, LANES), lambda b, c: (0, c))
    gate_spec = pl.BlockSpec((1, LANES, LANES), lambda b, c: (c, 0, 0))
    kernel = functools.partial(_lru_kernel, n_rows=n_rows, chunk=chunk)
    return pl.pallas_call(
        kernel,
        grid=(n_streams, heads),
        in_specs=[seq_spec, seq_spec,
                  pl.BlockSpec((1, CONV_WIDTH - 1, LANES), state_map), pl.BlockSpec((1, 1, LANES), state_map),
                  pl.BlockSpec((CONV_WIDTH, LANES), lambda b, c: (0, c)), vec_spec,
                  gate_spec, vec_spec, gate_spec, vec_spec, vec_spec],
        out_specs=[seq_spec,
                   pl.BlockSpec((1, CONV_WIDTH - 1, LANES), lambda b, c: (b, 0, c)),
                   pl.BlockSpec((1, 1, LANES), lambda b, c: (b, 0, c))],
        out_shape=[jax.ShapeDtypeStruct(xr.shape, BF16),
                   jax.ShapeDtypeStruct((n_streams, CONV_WIDTH - 1, width), F32),
                   jax.ShapeDtypeStruct((n_streams, 1, width), F32)],
        scratch_shapes=[pltpu.VMEM((n_rows + SUBLANES, LANES), F32), pltpu.VMEM((1, LANES), F32)],
        compiler_params=_params("parallel", "parallel"),
        name="conv_rglru",
    )(xr, yg, buf, h0, conv_w, conv_b.reshape(1, width), w_a, b_a.reshape(1, width), w_x, b_x.reshape(1, width),
      lam.reshape(1, width))


def _mix_kernel(attn_ref, m_ref, gs_ref, gl_ref, bgs_ref, bgl_ref, x_ref, wpa_ref, wpl_ref, wo_ref, nf_ref,
                x1_ref, xn_ref):
    y_sb = jnp.dot(attn_ref[...], wpa_ref[...], preferred_element_type=F32)
    y_lru = jnp.dot(m_ref[...], wpl_ref[...], preferred_element_type=F32)
    g_sb = jax.nn.sigmoid(gs_ref[...] + bgs_ref[...])
    g_lru = jax.nn.sigmoid(gl_ref[...] + bgl_ref[...])
    mix = (g_sb * y_sb + g_lru * y_lru).astype(BF16)
    x1 = x_ref[...] + jnp.dot(mix, wo_ref[...], preferred_element_type=F32)
    x1_ref[...] = x1
    ms = jnp.mean(x1 * x1, axis=-1, keepdims=True)
    xn_ref[...] = (x1 * lax.rsqrt(ms + RMS_EPS) * nf_ref[...]).astype(xn_ref.dtype)


def _mix_project(attn, m, gs, gl, b_gate, x, w_pa, w_pl, w_o, norm_ffn):
    n, d = x.shape
    tm = _row_block(n, 256, 16)
    row = lambda i: (i, 0)
    fixed = lambda i: (0, 0)
    blk = pl.BlockSpec((tm, d), row)
    vec = pl.BlockSpec((1, d), fixed)
    wspec = pl.BlockSpec((d, d), fixed, pipeline_mode=pl.Buffered(1))
    return pl.pallas_call(
        _mix_kernel,
        grid=(n // tm,),
        in_specs=[blk, blk, blk, blk, vec, vec, blk, wspec, wspec, wspec, vec],
        out_specs=[blk, blk],
        out_shape=[jax.ShapeDtypeStruct((n, d), F32), jax.ShapeDtypeStruct((n, d), BF16)],
        compiler_params=_params("parallel"),
        name="mix_project",
    )(attn, m, gs, gl, b_gate[:d].reshape(1, d), b_gate[d:].reshape(1, d), x, w_pa, w_pl, w_o,
      norm_ffn.reshape(1, d))


def _top_rows(s, k):
    n = s.shape[0]
    rows = lax.broadcasted_iota(jnp.int32, s.shape, 0)
    vals, idxs = [], []
    for _ in range(k):
        m = jnp.max(s, axis=0, keepdims=True)
        idx = jnp.min(jnp.where(s == m, rows, n), axis=0, keepdims=True)
        s = jnp.where(rows == idx, -jnp.inf, s)
        vals.append(m)
        idxs.append(idx)
        yield
    return jnp.concatenate(vals, axis=0), jnp.concatenate(idxs, axis=0)


def _finish(stages):
    while True:
        try:
            next(stages)
        except StopIteration as done:
            return done.value


def _take_rows(table, idx):
    out = jnp.zeros(idx.shape, table.dtype)
    for r in range(table.shape[0]):
        out = jnp.where(idx == r, table[r:r + 1, :], out)
    return out


def _route_unit(q1, q2, k1, k2):
    t1, i1 = yield from _top_rows(lax.dot_general(k1, q1, _NT, preferred_element_type=F32), PEER_TOPK)
    t2, i2 = yield from _top_rows(lax.dot_general(k2, q2, _NT, preferred_element_type=F32), PEER_TOPK)
    cand = jnp.concatenate(
        [t1[0:1, :] + t2]
        + [t1[a:a + 1, :] + t2[0:8, :] for a in range(1, 4)]
        + [t1[a:a + 1, :] + t2[0:4, :] for a in range(4, 8)]
        + [t1[8:16, :] + t2[0:1, :]], axis=0)
    score, r = yield from _top_rows(cand, PEER_TOPK)
    rank1 =jnp.where(r < 16, 0, jnp.where(r < 40, 1 + lax.shift_right_logical(r - 16, 3),
                                           jnp.where(r < 56, 4 + lax.shift_right_logical(r - 40, 2), r - 48)))
    rank2 = jnp.where(r < 16, r, jnp.where(r < 40, jnp.bitwise_and(r - 16, 7),
                                           jnp.where(r < 56, jnp.bitwise_and(r - 40, 3), 0)))
    e1 = _take_rows(i1, rank1)
    e2 = _take_rows(i2, rank2)
    ex = jnp.exp(score - jnp.max(score, axis=0, keepdims=True))
    gate = ex / jnp.sum(ex, axis=0, keepdims=True)
    return e1.astype(F32), e2.astype(F32), gate


def _query_kernel(x_ref, w_ref, o_ref):
    q = jnp.dot(x_ref[...], w_ref[...], preferred_element_type=F32).astype(o_ref.dtype)
    for part in range(o_ref.shape[0]):
        o_ref[part] = q[:, part * LANES:(part + 1) * LANES]


def _query_proj(xn, wq):
    n, d = xn.shape
    parts = wq.shape[1] // LANES
    tm = _row_block(n, 512, 16)
    return pl.pallas_call(
        _query_kernel,
        grid=(n // tm,),
        in_specs=[pl.BlockSpec((tm, d), lambda i: (i, 0)),
                  pl.BlockSpec(wq.shape, lambda i: (0, 0), pipeline_mode=pl.Buffered(1))],
        out_specs=pl.BlockSpec((parts, tm, LANES), lambda i: (0, i, 0)),
        out_shape=jax.ShapeDtypeStruct((parts, n, LANES), BF16),
        compiler_params=_params("parallel"),
        name="query_proj",
    )(xn, wq)


def _route_kernel(q_ref, k1_ref, k2_ref, e1_ref, e2_ref, g_ref, e1s_ref, e2s_ref, gs_ref):
    k1 = k1_ref[...]
    k2 = k2_ref[...]

    def head(h, carry):
        e1, e2, gate = _finish(_route_unit(q_ref[2 * h], q_ref[2 * h + 1], k1, k2))
        out0 = pl.multiple_of(h * PEER_TOPK, PEER_TOPK)
        e1s_ref[pl.ds(out0, PEER_TOPK), :] = e1
        e2s_ref[pl.ds(out0, PEER_TOPK), :] = e2
        gs_ref[pl.ds(out0, PEER_TOPK), :] = gate
        return carry

    lax.fori_loop(0, PEER_HEADS, head, 0)
    e1_ref[...] = e1s_ref[...].T
    e2_ref[...] = e2s_ref[...].T
    g_ref[...] = gs_ref[...].T


def _peer_route_first(q3, k1, k2, n_first):
    parts = q3.shape[0]
    slots = PEER_HEADS * PEER_TOPK
    fixed = lambda i: (0, 0)
    out = pl.BlockSpec((LANES, slots), lambda i: (i, 0))
    return pl.pallas_call(
        _route_kernel,
        grid=(n_first // LANES,),
        in_specs=[pl.BlockSpec((parts, LANES, LANES), lambda i: (0, i, 0)),
                  pl.BlockSpec(k1.shape, fixed), pl.BlockSpec(k2.shape, fixed)],
        out_specs=[out, out, out],
        out_shape=[jax.ShapeDtypeStruct((n_first, slots), F32)] * 3,
        scratch_shapes=[pltpu.VMEM((slots, LANES), F32)] * 3,
        compiler_params=_params("parallel"),
        name="peer_route_first",
    )(q3, k1, k2)


PEER_CHUNK_KEYS = 2


def _gate_tile_bits(n, keys, e1_ref, e2_ref, g_ref):
    first = jnp.where(keys == e1_ref[pl.ds(n, 1), :], g_ref[pl.ds(n, 1), :], 0.0).astype(BF16)
    second = jnp.where(keys == e2_ref[pl.ds(n, 1), :], 1.0, 0.0).astype(BF16)
    c = lax.dot_general(first, second, _NT, preferred_element_type=F32)
    return lax.bitcast_convert_type(c.astype(BF16).astype(F32), jnp.uint32)


def _peer_kernel(xn_ref, qn_ref, e1f_ref, e2f_ref, gf_ref, k1_ref, k2_ref, u0_ref, ua_ref, ub_ref, v_ref, x1_ref,
                 nf_ref, y_ref, c_ref, hid_a_ref, hid_b_ref, e1_ref, e2_ref, g_ref, e1n_ref, e2n_ref, gn_ref, *, tn):
    i = pl.program_id(0)
    j = pl.program_id(1)
    n_steps = pl.num_programs(1)
    half = tn // 2
    lane_blocks = tn // LANES
    ck = PEER_CHUNK_KEYS
    high = jnp.uint32(0xFFFF0000)

    @pl.when(j == 0)
    def _():
        @pl.when(i == 0)
        def _():
            e1_ref[...] = e1f_ref[...]
            e2_ref[...] = e2f_ref[...]
            g_ref[...] = gf_ref[...]

        y_ref[...] = jnp.zeros_like(y_ref)
        keys = lax.broadcasted_iota(jnp.int32, (N_KEYS, LANES), 0).astype(F32)

        def build(p, carry):
            bits = _gate_tile_bits(p, keys, e1_ref, e2_ref, g_ref)
            bits = bits | (_gate_tile_bits(p + half, keys, e1_ref, e2_ref, g_ref) >> 16)
            c_ref[pl.ds(pl.multiple_of(p * C_PITCH, 4), N_KEYS), :] = bits
            return carry

        lax.fori_loop(0, half, build, 0, unroll=32)
        hid_a_ref[...] = lax.dot_general(xn_ref[...], u0_ref[...], _NT, preferred_element_type=F32)

    def weighted(hid_ref, first_key):
        parts = []
        for t in range(ck):
            bits = c_ref[pl.ds(first_key + t, half, stride=C_PITCH), :]
            c = jnp.concatenate([lax.bitcast_convert_type(bits & high, F32),
                                 lax.bitcast_convert_type(bits << 16, F32)], axis=0)
            parts.append((c * jax.nn.gelu(hid_ref[:, t * N_KEYS:(t + 1) * N_KEYS])).astype(BF16))
        return parts

    head = (j // lane_blocks) % PEER_HEADS
    lane_block = j % lane_blocks
    tok0 = pl.multiple_of(lane_block * LANES, LANES)
    xn = xn_ref[...]
    rows = ck * N_KEYS
    routing = _route_unit(qn_ref[2 * head, pl.ds(tok0, LANES), :], qn_ref[2 * head + 1, pl.ds(tok0, LANES), :],
                          k1_ref[...], k2_ref[...])

    def route_some(count):
        for _ in range(count):
            next(routing, None)

    def add_weighted(w, v_rows, extractions):
        for col in range(0, y_ref.shape[1], MXU_COLS):
            y_ref[:, col:col + MXU_COLS] += jnp.dot(w, v_ref[v_rows:v_rows + rows, col:col + MXU_COLS],
                                                    preferred_element_type=F32)
            route_some(extractions)

    w_a = jnp.concatenate(weighted(hid_a_ref, 2 * ck * j), axis=1)
    hid_b_ref[...] = lax.dot_general(xn, ua_ref[...], _NT, preferred_element_type=F32)
    route_some(4)
    add_weighted(w_a, 0, 2)
    w_b = jnp.concatenate(weighted(hid_b_ref, 2 * ck * j + ck), axis=1)
    hid_a_ref[...] = lax.dot_general(xn, ub_ref[...], _NT, preferred_element_type=F32)
    route_some(4)
    add_weighted(w_b, rows, 3)
    e1n, e2n, gn = _finish(routing)

    slot0 = pl.multiple_of(head * PEER_TOPK, PEER_TOPK)
    e1n_ref[lane_block, pl.ds(slot0, PEER_TOPK), :] = e1n
    e2n_ref[lane_block, pl.ds(slot0, PEER_TOPK), :] = e2n
    gn_ref[lane_block, pl.ds(slot0, PEER_TOPK), :] = gn

    @pl.when(j == n_steps - 1)
    def _():
        x2 = x1_ref[...] + y_ref[...]
        ms = jnp.mean(x2 * x2, axis=-1, keepdims=True)
        y_ref[...] = x2 * lax.rsqrt(ms + RMS_EPS) * nf_ref[...]
        for lb in range(lane_blocks):
            e1_ref[lb * LANES:(lb + 1) * LANES, :] = e1n_ref[lb].T
            e2_ref[lb * LANES:(lb + 1) * LANES, :] = e2n_ref[lb].T
            g_ref[lb * LANES:(lb + 1) * LANES, :] = gn_ref[lb].T


def _peer_experts(xn, q3, e1_first, e2_first, g_first, k1, k2, u, v, x1, norm_final):
    n, d = xn.shape
    tn = e1_first.shape[0]
    rows = PEER_CHUNK_KEYS * N_KEYS
    n_chunks = u.shape[0] // rows
    n_blocks = n // tn
    slots = PEER_HEADS * PEER_TOPK
    assert n_chunks % 2 == 0 and n % tn == 0 and tn % LANES == 0
    assert n_chunks // 2 >= PEER_HEADS * (tn // LANES), "one routing unit per grid step must cover the next block"
    blk = lambda i, j: (i, 0)
    fixed = lambda i, j: (0, 0)
    first = pl.BlockSpec((tn, slots), fixed)
    kernel = functools.partial(_peer_kernel, tn=tn)
    return pl.pallas_call(
        kernel,
        grid=(n_blocks, n_chunks // 2),
        in_specs=[pl.BlockSpec((tn, d), blk, pipeline_mode=pl.Buffered(1)),
                  pl.BlockSpec((q3.shape[0], tn, LANES), lambda i, j: (0, jnp.minimum(i + 1, n_blocks - 1), 0)),
                  first, first, first,
                  pl.BlockSpec(k1.shape, fixed), pl.BlockSpec(k2.shape, fixed),
                  pl.BlockSpec((rows, d), fixed, pipeline_mode=pl.Buffered(1)),
                  pl.BlockSpec((rows, d), lambda i, j: (2 * j + 1, 0)),
                  pl.BlockSpec((rows, d), lambda i, j: (jnp.minimum(2 * j + 2, n_chunks - 1), 0)),
                  pl.BlockSpec((2 * rows, d), lambda i, j: (j, 0)),
                  pl.BlockSpec((tn, d), blk, pipeline_mode=pl.Buffered(1)),
                  pl.BlockSpec((1, d), fixed)],
        out_specs=pl.BlockSpec((tn, d), blk),
        out_shape=jax.ShapeDtypeStruct((n, d), F32),
        scratch_shapes=[pltpu.VMEM((tn // 2 * C_PITCH, LANES), jnp.uint32),
                        pltpu.VMEM((tn, rows), F32), pltpu.VMEM((tn, rows), F32)]
                       + [pltpu.VMEM((tn, slots), F32)] * 3
                       + [pltpu.VMEM((tn // LANES, slots, LANES), F32)] * 3,
        compiler_params=_params("arbitrary", "arbitrary"),
        name="peer_experts",
    )(xn, q3, e1_first, e2_first, g_first, k1, k2, u, u, u, v, x1, norm_final.reshape(1, d))


def kernel(x_prompt, x_sample, cache_sb_k, cache_sb_v, state_conv, state_lru, meta_tokens, norm_mix, norm_ffn, w_in, b_gate, conv_w, conv_b, w_rg_a, b_rg_a, w_rg_x, b_rg_x, lru_lambda, w_proj_attn, w_proj_lru, w_out, w_query, sub_keys, expert_u, expert_v, norm_final):
    depth = w_in.shape[0]
    assert depth == 1, "one layer: the meta tokens' residual stream is never read after the mixer state"
    batch, seq, d = x_prompt.shape
    dec_batch, dec_seq, _ = x_sample.shape
    n_cache = cache_sb_k.shape[2]
    heads = cache_sb_k.shape[3]
    n_frames = batch * seq
    n_sample = dec_batch * dec_seq
    n_meta = meta_tokens.shape[0]
    assert n_meta == N_META and d == heads * LANES

    l = 0
    w_in_b = w_in[l].astype(BF16)
    w_a = w_rg_a[l].astype(BF16)
    w_x = w_rg_x[l].astype(BF16)
    w_pa = w_proj_attn[l].astype(BF16)
    w_pl = w_proj_lru[l].astype(BF16)
    w_o = w_out[l].astype(BF16)
    wq_b = w_query[l].astype(BF16)
    k1 = sub_keys[l, 0].astype(BF16)
    k2 = sub_keys[l, 1].astype(BF16)
    u_b = expert_u[l].astype(BF16)
    v_b = expert_v[l].astype(BF16)

    x_f = x_prompt.reshape(n_frames, d)
    x_s = jnp.concatenate([x_sample.reshape(n_sample, d), meta_tokens.astype(x_sample.dtype)], axis=0)

    col_q, col_k, col_v = 0, 1, 2
    other_cols = (3, 4, 5, 6)

    def project(xn, cols, dtype):
        return [_matmul_cols(xn, w_in_b, g, d, dtype) for g in cols]

    xn_f, q_f = _norm_project(x_f, norm_mix[l], w_in_b, col_q, d, BF16)
    xn_s, q_s = _norm_project(x_s, norm_mix[l], w_in_b, col_q, d, BF16)
    xr_f, yg_f, gs_f, gl_f = project(xn_f, other_cols, F32)
    k_s, v_s, xr_s, yg_s, gs_s, gl_s = project(xn_s, (col_k, col_v) + other_cols, F32)
    xn_meta = jnp.broadcast_to(xn_s[n_sample:][None], (batch, n_meta, d))
    xn_p = jnp.concatenate([xn_meta, xn_f.reshape(batch, seq, d)], axis=1).reshape(batch * (n_meta + seq), d)
    k_p, v_p = project(xn_p, (col_k, col_v), F32)

    lru_w = (conv_w[l], conv_b[l], w_a, b_rg_a[l], w_x, b_rg_x[l], lru_lambda[l])
    _, buf_meta, h_meta = _conv_rglru(
        xr_s[n_sample:], yg_s[n_sample:], jnp.zeros((1, CONV_WIDTH - 1, d), F32), jnp.zeros((1, 1, d), F32),
        *lru_w, n_streams=1, n_rows=n_meta, shared_state=True)
    m_f, buf_f, h_f = _conv_rglru(xr_f, yg_f, buf_meta, h_meta, *lru_w,
                                  n_streams=batch, n_rows=seq, shared_state=True)
    m_s, buf_s, h_s = _conv_rglru(xr_s[:n_sample], yg_s[:n_sample], state_conv[l], state_lru[l][:, None, :],
                                  *lru_w, n_streams=dec_batch, n_rows=dec_seq, shared_state=False)

    attn_f = _sb_attention(q_f, k_p, v_p, n_streams=batch, n_new=seq, n_cache=n_meta, heads=heads)
    attn_s = _sb_attention(q_s[:n_sample],
                           (k_s[:n_sample], cache_sb_k[l].reshape(dec_batch * n_cache, d)),
                           (v_s[:n_sample], cache_sb_v[l].reshape(dec_batch * n_cache, d)),
                           n_streams=dec_batch, n_new=dec_seq, n_cache=n_cache, heads=heads)

    def finish(attn, m, gs, gl, x):
        x1, xn2 = _mix_project(attn, m, gs, gl, b_gate[l], x, w_pa, w_pl, w_o, norm_ffn[l])
        q3 = _query_proj(xn2, wq_b)
        tn = _row_block(xn2.shape[0], 512, LANES)
        e1, e2, gate = _peer_route_first(q3, k1, k2, tn)
        return _peer_experts(xn2, q3, e1, e2, gate, k1, k2, u_b, v_b, x1, norm_final)

    y_f = finish(attn_f, m_f, gs_f, gl_f, x_f)
    y_s = finish(attn_s, m_s[:n_sample], gs_s[:n_sample], gl_s[:n_sample], x_s[:n_sample])

    return (y_f.reshape(batch, seq, d), y_s.reshape(dec_batch, dec_seq, d),
            k_p.reshape(1, batch, n_meta + seq, heads, LANES), v_p.reshape(1, batch, n_meta + seq, heads, LANES),
            buf_f[None], h_f.reshape(1, batch, d),
            k_s[:n_sample].reshape(1, dec_batch, dec_seq, heads, LANES),
            v_s[:n_sample].reshape(1, dec_batch, dec_seq, heads, LANES),
            buf_s[None], h_s.reshape(1, dec_batch, d))
```

```python
import functools
import math

import jax
import jax.numpy as jnp
from jax import lax
from jax.experimental import pallas as pl
from jax.experimental.pallas import tpu as pltpu

F32 = jnp.float32
BF16 = jnp.bfloat16

LANES = 128
SUBLANES = 8
VMEM_LIMIT_BYTES = 56 * 1024 * 1024
MXU_COLS = 256

RMS_EPS = 1e-6
RG_C = 8.0
CONV_WIDTH = 4
N_META = 16
PEER_TOPK = 16
PEER_HEADS = 8
N_KEYS = 128
SB_LOG_CUTOFF = -105.0
SB_MASKED = -1e30
SB_CHAINS = 8
SB_FIRST_BLOCKS = 3
SB_LOOP_BLOCKS = 2
SB_SHORT_STREAM_HEADS = 4
C_PITCH = 132

_NT = (((1,), (1,)), ((), ()))


def _params(*sem):
    return pltpu.CompilerParams(dimension_semantics=sem, vmem_limit_bytes=VMEM_LIMIT_BYTES)


def _row_block(n, target, align):
    best = None
    for t in range(align, min(n, target) + 1, align):
        if n % t == 0:
            best = t
    assert best is not None, (n, target, align)
    return best


def _softplus_neg_abs(z):
    return jnp.log1p(jnp.exp(-jnp.abs(z)))


def _split_bf16(x):
    hi = x.astype(BF16)
    lo = (x - hi.astype(F32)).astype(BF16)
    return hi, lo


def _norm_project_kernel(x_ref, g_ref, w_ref, xn_ref, o_ref):
    x = x_ref[...]
    ms = jnp.mean(x * x, axis=-1, keepdims=True)
    xn = (x * lax.rsqrt(ms + RMS_EPS) * g_ref[...]).astype(xn_ref.dtype)
    xn_ref[...] = xn
    o_ref[...] = jnp.dot(xn, w_ref[...], preferred_element_type=F32).astype(o_ref.dtype)


def _norm_project(x, g, w, col_block, ncols, out_dtype):
    n, d = x.shape
    tm = _row_block(n, 512, 16)
    rows = lambda i: (i, 0)
    return pl.pallas_call(
        _norm_project_kernel,
        grid=(n // tm,),
        in_specs=[pl.BlockSpec((tm, d), rows), pl.BlockSpec((1, d), lambda i: (0, 0)),
                  pl.BlockSpec((d, ncols), lambda i: (0, col_block), pipeline_mode=pl.Buffered(1))],
        out_specs=[pl.BlockSpec((tm, d), rows), pl.BlockSpec((tm, ncols), rows)],
        out_shape=[jax.ShapeDtypeStruct((n, d), BF16), jax.ShapeDtypeStruct((n, ncols), out_dtype)],
        compiler_params=_params("parallel"),
        name="norm_project",
    )(x, g.reshape(1, d), w)


def _matmul_kernel(x_ref, w_ref, o_ref):
    o_ref[...] = jnp.dot(x_ref[...], w_ref[...], preferred_element_type=F32).astype(o_ref.dtype)


def _matmul_cols(x, w, col_block, ncols, out_dtype):
    n, k = x.shape
    tm = _row_block(n, 640, 16)
    return pl.pallas_call(
        _matmul_kernel,
        grid=(n // tm,),
        in_specs=[
            pl.BlockSpec((tm, k), lambda i: (i, 0)),
            pl.BlockSpec((k, ncols), lambda i: (0, col_block), pipeline_mode=pl.Buffered(1)),
        ],
        out_specs=pl.BlockSpec((tm, ncols), lambda i: (i, 0)),
        out_shape=jax.ShapeDtypeStruct((n, ncols), out_dtype),
        compiler_params=_params("parallel"),
        name="in_proj",
    )(x, w)


def _sb_kernel(*refs, heads_per_step, **static):
    *operands, kb_ref, vb_ref = refs
    for h in range(heads_per_step):
        lanes = pl.ds(h * LANES, LANES)
        _sb_head(*[ref.at[:, lanes] for ref in operands], kb_ref, vb_ref, **static)


def _sb_head(*refs, n_new, n_cache, q_rows, chains, joint, scale):
    q_ref = refs[0]
    o_ref, kb_ref, vb_ref = refs[-3:]
    first_block = SB_LOOP_BLOCKS
    head0 = first_block * LANES + LANES - N_META
    new0 = head0 + n_cache
    n_scratch = kb_ref.shape[0]

    def fill(src_refs, n_rows, dst0):
        def chunk(c, carry):
            src = pl.multiple_of(c * LANES, LANES)
            dst = pl.multiple_of(dst0 + c * LANES, 16)
            for src_ref, dst_ref in zip(src_refs, (kb_ref, vb_ref)):
                dst_ref[pl.ds(dst, LANES), :] = src_ref[pl.ds(src, LANES), :].astype(BF16)
            return carry

        full = n_rows // LANES
        if full:
            lax.fori_loop(0, full, chunk, 0)
        if n_rows % LANES:
            for src_ref, dst_ref in zip(src_refs, (kb_ref, vb_ref)):
                dst_ref[dst0 + full * LANES:dst0 + n_rows, :] = src_ref[full * LANES:n_rows, :].astype(BF16)

    zeros_head = jnp.zeros((head0, LANES), BF16)
    kb_ref[0:head0, :] = zeros_head
    vb_ref[0:head0, :] = zeros_head
    if joint:
        fill(refs[1:3], n_cache + n_new, head0)
    else:
        fill(refs[3:5], n_cache, head0)
        fill(refs[1:3], n_new, new0)
    if new0 + n_new < n_scratch:
        tail = jnp.zeros((n_scratch - new0 - n_new, LANES), BF16)
        kb_ref[new0 + n_new:n_scratch, :] = tail
        vb_ref[new0 + n_new:n_scratch, :] = tail

    row = lax.broadcasted_iota(jnp.int32, (q_rows, LANES), 0)
    col = lax.broadcasted_iota(jnp.int32, (q_rows, LANES), 1)

    def later_matrix(width):
        kr = lax.broadcasted_iota(jnp.int32, (width, width), 0)
        kc = lax.broadcasted_iota(jnp.int32, (width, width), 1)
        return jnp.concatenate([jnp.where(kr > kc, 1.0, 0.0), jnp.ones((width, LANES), F32)], axis=1).astype(BF16)

    later = {n: later_matrix(n * LANES) for n in {SB_FIRST_BLOCKS, SB_LOOP_BLOCKS}}

    def segments(qbs, blocks, n_blocks, biases, carries, accs):
        width = n_blocks * LANES
        zs, sps, logs = [], [], []
        for qb, blk, bias in zip(qbs, blocks, biases):
            s0 = pl.multiple_of(blk * LANES, LANES)
            z = lax.dot_general(qb, kb_ref[pl.ds(s0, width), :], _NT, preferred_element_type=F32) * scale + bias
            sp = jnp.log(1.0 + jnp.exp(-jnp.abs(z)))
            zs.append(z)
            sps.append(sp)
            logs.append(-(jnp.maximum(z, 0.0) + sp))
        hi, lo = _split_bf16(jnp.concatenate(logs, axis=0))
        sums = jnp.dot(jnp.concatenate([hi, lo], axis=0), later[n_blocks], preferred_element_type=F32)
        sums = sums[:chains * q_rows] + sums[chains * q_rows:]
        new_carries, new_accs = [], []
        for c in range(chains):
            s = sums[c * q_rows:(c + 1) * q_rows]
            w = jnp.exp(jnp.minimum(zs[c], 0.0) - sps[c] + (s[:, :width] + jnp.tile(carries[c], (1, n_blocks))))
            v0 = pl.multiple_of(blocks[c] * LANES, LANES)
            new_accs.append(accs[c] + jnp.dot(w.astype(BF16), vb_ref[pl.ds(v0, width), :],
                                              preferred_element_type=F32))
            new_carries.append(carries[c] + s[:, width:])
        return tuple(new_carries), tuple(new_accs)

    causal_bias = jnp.where(col < row, 0.0, SB_MASKED)
    not_first_positions = jnp.where(col >= LANES - N_META, 0.0, 1.0)

    def block_bias(block):
        return (not_first_positions * jnp.where(block == first_block, SB_MASKED, 0.0)
                + jnp.where(block < first_block, SB_MASKED, 0.0))

    def query_group(g, carry_unused):
        q0s = [pl.multiple_of((g * chains + c) * q_rows, q_rows) for c in range(chains)]
        qbs = [q_ref[pl.ds(q0, q_rows), :] for q0 in q0s]
        diag = [(new0 + (g * chains + c) * q_rows) // LANES for c in range(chains)]
        zero = jnp.zeros((q_rows, LANES), F32)
        first = [diag[c] - (SB_FIRST_BLOCKS - 1) for c in range(chains)]
        biases = [jnp.concatenate([block_bias(first[c] + k) for k in range(SB_FIRST_BLOCKS - 1)] + [causal_bias],
                                  axis=1) for c in range(chains)]
        carries, accs = segments(qbs, first, SB_FIRST_BLOCKS, biases, (zero,) * chains, (zero,) * chains)

        def oldest(c, trip):
            return first[c] - SB_LOOP_BLOCKS * (trip + 1)

        def cond(state):
            trip, carries, _ = state
            go = jnp.bool_(False)
            for c in range(chains):
                alive = oldest(c, trip) + SB_LOOP_BLOCKS - 1 >= first_block
                go = jnp.logical_or(go, jnp.logical_and(alive, jnp.max(carries[c]) > SB_LOG_CUTOFF))
            return go

        def body(state):
            trip, carries, accs = state
            blocks = [jnp.maximum(oldest(c, trip), 0) for c in range(chains)]
            biases = [jnp.concatenate([block_bias(blocks[c] + k) for k in range(SB_LOOP_BLOCKS)], axis=1)
                      for c in range(chains)]
            carries, accs = segments(qbs, blocks, SB_LOOP_BLOCKS, biases, carries, accs)
            return trip + 1, carries, accs

        _, _, accs = lax.while_loop(cond, body, (jnp.int32(0), carries, accs))
        for c in range(chains):
            o_ref[pl.ds(q0s[c], q_rows), :] = accs[c].astype(o_ref.dtype)
        return carry_unused

    lax.fori_loop(0, n_new // (q_rows * chains), query_group, 0)


def _sb_attention(q, keys, values, *, n_streams, n_new, n_cache, heads):
    assert (n_cache - N_META) % LANES == 0
    joint = not isinstance(keys, tuple)
    q_rows = min(n_new, LANES)
    assert n_new % q_rows == 0 and q_rows % 16 == 0
    new0 = (SB_LOOP_BLOCKS + 1) * LANES + (n_cache - N_META)
    n_scratch = new0 + (n_new // q_rows - 1) * q_rows + LANES
    heads_per_step = SB_SHORT_STREAM_HEADS if n_new < LANES else 1
    assert heads % heads_per_step == 0
    width = heads_per_step * LANES
    stream_head = lambda b, h: (b, h)
    new_spec = pl.BlockSpec((n_new, width), stream_head)
    if joint:
        kv_specs = [pl.BlockSpec((n_cache + n_new, width), stream_head)] * 2
        kv_args = (keys, values)
    else:
        kv_specs = [new_spec, new_spec] + [pl.BlockSpec((n_cache, width), stream_head)] * 2
        kv_args = (keys[0], values[0], keys[1], values[1])
    n_qblocks = n_new // q_rows
    chains = SB_CHAINS if n_qblocks % SB_CHAINS == 0 else 1
    kernel = functools.partial(_sb_kernel, heads_per_step=heads_per_step, n_new=n_new, n_cache=n_cache,
                               q_rows=q_rows, chains=chains, joint=joint, scale=1.0 / math.sqrt(LANES))
    return pl.pallas_call(
        kernel,
        grid=(n_streams, heads // heads_per_step),
        in_specs=[new_spec] + kv_specs,
        out_specs=new_spec,
        out_shape=jax.ShapeDtypeStruct(q.shape, BF16),
        scratch_shapes=[pltpu.VMEM((n_scratch, LANES), BF16), pltpu.VMEM((n_scratch, LANES), BF16)],
        compiler_params=_params("parallel", "parallel"),
        name="sb_attention",
    )(q, *kv_args)


def _lru_kernel(xr_ref, yg_ref, buf_ref, h0_ref, cw_ref, cb_ref, wa_ref, ba_ref, wx_ref, bx_ref, lam_ref,
                m_ref, nbuf_ref, hl_ref, xp_ref, h_ref, *, n_rows, chunk):
    pad = SUBLANES
    n_hist = CONV_WIDTH - 1
    xp_ref[pad - n_hist:pad, :] = buf_ref[0]

    def copy(c, carry):
        r0 = pl.multiple_of(c * chunk, chunk)
        xp_ref[pl.ds(r0 + pad, chunk), :] = xr_ref[pl.ds(r0, chunk), :]
        return carry

    lax.fori_loop(0, n_rows // chunk, copy, 0)
    nbuf_ref[0] = xp_ref[n_rows + pad - n_hist:n_rows + pad, :]

    lam = lam_ref[...]
    sp_lam = jnp.maximum(-lam, 0.0) + _softplus_neg_abs(lam)
    cw = cw_ref[...]
    cb = cb_ref[...]
    wa = wa_ref[0]
    wx = wx_ref[0]
    ba = ba_ref[...]
    bx = bx_ref[...]
    groups = chunk // SUBLANES
    sub = lax.broadcasted_iota(jnp.int32, (groups, SUBLANES, LANES), 1)
    h_ref[...] = h0_ref[0]

    def step(c, carry):
        r0 = pl.multiple_of(c * chunk, chunk)
        xc = cb
        for tap in range(CONV_WIDTH):
            xc = xc + cw[tap:tap + 1, :] * xp_ref[pl.ds(r0 + pad - n_hist + tap, chunk), :]
        xcb = xc.astype(BF16)
        r = jax.nn.sigmoid(jnp.dot(xcb, wa, preferred_element_type=F32) + ba)
        gate_i = jax.nn.sigmoid(jnp.dot(xcb, wx, preferred_element_type=F32) + bx)
        log_a = -RG_C * r * sp_lam
        a = jnp.exp(log_a)
        t = jnp.tanh(log_a)
        u = jnp.sqrt(-2.0 * t / (1.0 - t)) * (gate_i * xc)
        a3 = a.reshape(groups, SUBLANES, LANES)
        u3 = u.reshape(groups, SUBLANES, LANES)
        shift = 1
        while shift < SUBLANES:
            keep = sub >= shift
            u3 = jnp.where(keep, a3 * pltpu.roll(u3, shift, axis=1) + u3, u3)
            a3 = jnp.where(keep, a3 * pltpu.roll(a3, shift, axis=1), a3)
            shift *= 2
        h = h_ref[...]
        rows = []
        for g in range(groups):
            hg = a3[g] * h + u3[g]
            h = hg[SUBLANES - 1:SUBLANES, :]
            rows.append(hg)
        h_ref[...] = h
        hs = jnp.concatenate(rows, axis=0)
        m_ref[pl.ds(r0, chunk), :] = (jax.nn.gelu(yg_ref[pl.ds(r0, chunk), :]) * hs).astype(m_ref.dtype)
        return carry

    n_chunks = n_rows // chunk
    lax.fori_loop(0, n_chunks, step, 0, unroll=8 if n_chunks % 8 == 0 else 1)
    hl_ref[0] = h_ref[...]


def _conv_rglru(xr, yg, buf, h0, conv_w, conv_b, w_a, b_a, w_x, b_x, lam, *, n_streams, n_rows, shared_state):
    width = xr.shape[1]
    heads = width // LANES
    chunk = _row_block(n_rows, LANES, 16)
    state_map = (lambda b, c: (0, 0, c)) if shared_state else (lambda b, c: (b, 0, c))
    seq_spec = pl.BlockSpec((n_rows, LANES), lambda b, c: (b, c))
    vec_spec = pl.BlockSpec((1, LANES), lambda b, c: (0, c))
    gate_spec = pl.BlockSpec((1, LANES, LANES), lambda b, c: (c, 0, 0))
    kernel = functools.partial(_lru_kernel, n_rows=n_rows, chunk=chunk)
    return pl.pallas_call(
        kernel,
        grid=(n_streams, heads),
        in_specs=[seq_spec, seq_spec,
                  pl.BlockSpec((1, CONV_WIDTH - 1, LANES), state_map), pl.BlockSpec((1, 1, LANES), state_map),
                  pl.BlockSpec((CONV_WIDTH, LANES), lambda b, c: (0, c)), vec_spec,
                  gate_spec, vec_spec, gate_spec, vec_spec, vec_spec],
        out_specs=[seq_spec,
                   pl.BlockSpec((1, CONV_WIDTH - 1, LANES), lambda b, c: (b, 0, c)),
                   pl.BlockSpec((1, 1, LANES), lambda b, c: (b, 0, c))],
        out_shape=[jax.ShapeDtypeStruct(xr.shape, BF16),
                   jax.ShapeDtypeStruct((n_streams, CONV_WIDTH - 1, width), F32),
                   jax.ShapeDtypeStruct((n_streams, 1, width), F32)],
        scratch_shapes=[pltpu.VMEM((n_rows + SUBLANES, LANES), F32), pltpu.VMEM((1, LANES), F32)],
        compiler_params=_params("parallel", "parallel"),
        name="conv_rglru",
    )(xr, yg, buf, h0, conv_w, conv_b.reshape(1, width), w_a, b_a.reshape(1, width), w_x, b_x.reshape(1, width),
      lam.reshape(1, width))


def _mix_kernel(attn_ref, m_ref, gs_ref, gl_ref, bgs_ref, bgl_ref, x_ref, wpa_ref, wpl_ref, wo_ref, nf_ref,
                x1_ref, xn_ref):
    y_sb = jnp.dot(attn_ref[...], wpa_ref[...], preferred_element_type=F32)
    y_lru = jnp.dot(m_ref[...], wpl_ref[...], preferred_element_type=F32)
    g_sb = jax.nn.sigmoid(gs_ref[...] + bgs_ref[...])
    g_lru = jax.nn.sigmoid(gl_ref[...] + bgl_ref[...])
    mix = (g_sb * y_sb + g_lru * y_lru).astype(BF16)
    x1 = x_ref[...] + jnp.dot(mix, wo_ref[...], preferred_element_type=F32)
    x1_ref[...] = x1
    ms = jnp.mean(x1 * x1, axis=-1, keepdims=True)
    xn_ref[...] = (x1 * lax.rsqrt(ms + RMS_EPS) * nf_ref[...]).astype(xn_ref.dtype)


def _mix_project(attn, m, gs, gl, b_gate, x, w_pa, w_pl, w_o, norm_ffn):
    n, d = x.shape
    tm = _row_block(n, 256, 16)
    row = lambda i: (i, 0)
    fixed = lambda i: (0, 0)
    blk = pl.BlockSpec((tm, d), row)
    vec = pl.BlockSpec((1, d), fixed)
    wspec = pl.BlockSpec((d, d), fixed, pipeline_mode=pl.Buffered(1))
    return pl.pallas_call(
        _mix_kernel,
        grid=(n // tm,),
        in_specs=[blk, blk, blk, blk, vec, vec, blk, wspec, wspec, wspec, vec],
        out_specs=[blk, blk],
        out_shape=[jax.ShapeDtypeStruct((n, d), F32), jax.ShapeDtypeStruct((n, d), BF16)],
        compiler_params=_params("parallel"),
        name="mix_project",
    )(attn, m, gs, gl, b_gate[:d].reshape(1, d), b_gate[d:].reshape(1, d), x, w_pa, w_pl, w_o,
      norm_ffn.reshape(1, d))


def _top_rows(s, k):
    n = s.shape[0]
    rows = lax.broadcasted_iota(jnp.int32, s.shape, 0)
    vals, idxs = [], []
    for _ in range(k):
        m = jnp.max(s, axis=0, keepdims=True)
        idx = jnp.min(jnp.where(s == m, rows, n), axis=0, keepdims=True)
        s = jnp.where(rows == idx, -jnp.inf, s)
        vals.append(m)
        idxs.append(idx)
        yield
    return jnp.concatenate(vals, axis=0), jnp.concatenate(idxs, axis=0)


def _finish(stages):
    while True:
        try:
            next(stages)
        except StopIteration as done:
            return done.value


def _take_rows(table, idx):
    out = jnp.zeros(idx.shape, table.dtype)
    for r in range(table.shape[0]):
        out = jnp.where(idx == r, table[r:r + 1, :], out)
    return out


def _route_unit(q1, q2, k1, k2):
    t1, i1 = yield from _top_rows(lax.dot_general(k1, q1, _NT, preferred_element_type=F32), PEER_TOPK)
    t2, i2 = yield from _top_rows(lax.dot_general(k2, q2, _NT, preferred_element_type=F32), PEER_TOPK)
    cand = jnp.concatenate(
        [t1[0:1, :] + t2]
        + [t1[a:a + 1, :] + t2[0:8, :] for a in range(1, 4)]
        + [t1[a:a + 1, :] + t2[0:4, :] for a in range(4, 8)]
        + [t1[8:16, :] + t2[0:1, :]], axis=0)
    score, r = yield from _top_rows(cand, PEER_TOPK)
    rank1 =jnp.where(r < 16, 0, jnp.where(r < 40, 1 + lax.shift_right_logical(r - 16, 3),
                                           jnp.where(r < 56, 4 + lax.shift_right_logical(r - 40, 2), r - 48)))
    rank2 = jnp.where(r < 16, r, jnp.where(r < 40, jnp.bitwise_and(r - 16, 7),
                                           jnp.where(r < 56, jnp.bitwise_and(r - 40, 3), 0)))
    e1 = _take_rows(i1, rank1)
    e2 = _take_rows(i2, rank2)
    ex = jnp.exp(score - jnp.max(score, axis=0, keepdims=True))
    gate = ex / jnp.sum(ex, axis=0, keepdims=True)
    return e1.astype(F32), e2.astype(F32), gate


def _query_kernel(x_ref, w_ref, o_ref):
    q = jnp.dot(x_ref[...], w_ref[...], preferred_element_type=F32).astype(o_ref.dtype)
    for part in range(o_ref.shape[0]):
        o_ref[part] = q[:, part * LANES:(part + 1) * LANES]


def _query_proj(xn, wq):
    n, d = xn.shape
    parts = wq.shape[1] // LANES
    tm = _row_block(n, 512, 16)
    return pl.pallas_call(
        _query_kernel,
        grid=(n // tm,),
        in_specs=[pl.BlockSpec((tm, d), lambda i: (i, 0)),
                  pl.BlockSpec(wq.shape, lambda i: (0, 0), pipeline_mode=pl.Buffered(1))],
        out_specs=pl.BlockSpec((parts, tm, LANES), lambda i: (0, i, 0)),
        out_shape=jax.ShapeDtypeStruct((parts, n, LANES), BF16),
        compiler_params=_params("parallel"),
        name="query_proj",
    )(xn, wq)


def _route_kernel(q_ref, k1_ref, k2_ref, e1_ref, e2_ref, g_ref, e1s_ref, e2s_ref, gs_ref):
    k1 = k1_ref[...]
    k2 = k2_ref[...]

    def head(h, carry):
        e1, e2, gate = _finish(_route_unit(q_ref[2 * h], q_ref[2 * h + 1], k1, k2))
        out0 = pl.multiple_of(h * PEER_TOPK, PEER_TOPK)
        e1s_ref[pl.ds(out0, PEER_TOPK), :] = e1
        e2s_ref[pl.ds(out0, PEER_TOPK), :] = e2
        gs_ref[pl.ds(out0, PEER_TOPK), :] = gate
        return carry

    lax.fori_loop(0, PEER_HEADS, head, 0)
    e1_ref[...] = e1s_ref[...].T
    e2_ref[...] = e2s_ref[...].T
    g_ref[...] = gs_ref[...].T


def _peer_route_first(q3, k1, k2, n_first):
    parts = q3.shape[0]
    slots = PEER_HEADS * PEER_TOPK
    fixed = lambda i: (0, 0)
    out = pl.BlockSpec((LANES, slots), lambda i: (i, 0))
    return pl.pallas_call(
        _route_kernel,
        grid=(n_first // LANES,),
        in_specs=[pl.BlockSpec((parts, LANES, LANES), lambda i: (0, i, 0)),
                  pl.BlockSpec(k1.shape, fixed), pl.BlockSpec(k2.shape, fixed)],
        out_specs=[out, out, out],
        out_shape=[jax.ShapeDtypeStruct((n_first, slots), F32)] * 3,
        scratch_shapes=[pltpu.VMEM((slots, LANES), F32)] * 3,
        compiler_params=_params("parallel"),
        name="peer_route_first",
    )(q3, k1, k2)


PEER_CHUNK_KEYS = 2


def _gate_tile_bits(n, keys, e1_ref, e2_ref, g_ref):
    first = jnp.where(keys == e1_ref[pl.ds(n, 1), :], g_ref[pl.ds(n, 1), :], 0.0).astype(BF16)
    second = jnp.where(keys == e2_ref[pl.ds(n, 1), :], 1.0, 0.0).astype(BF16)
    c = lax.dot_general(first, second, _NT, preferred_element_type=F32)
    return lax.bitcast_convert_type(c.astype(BF16).astype(F32), jnp.uint32)


def _peer_kernel(xn_ref, qn_ref, e1f_ref, e2f_ref, gf_ref, k1_ref, k2_ref, u0_ref, ua_ref, ub_ref, v_ref, x1_ref,
                 nf_ref, y_ref, c_ref, hid_a_ref, hid_b_ref, e1_ref, e2_ref, g_ref, e1n_ref, e2n_ref, gn_ref, *, tn):
    i = pl.program_id(0)
    j = pl.program_id(1)
    n_steps = pl.num_programs(1)
    half = tn // 2
    lane_blocks = tn // LANES
    ck = PEER_CHUNK_KEYS
    high = jnp.uint32(0xFFFF0000)

    @pl.when(j == 0)
    def _():
        @pl.when(i == 0)
        def _():
            e1_ref[...] = e1f_ref[...]
            e2_ref[...] = e2f_ref[...]
            g_ref[...] = gf_ref[...]

        y_ref[...] = jnp.zeros_like(y_ref)
        keys = lax.broadcasted_iota(jnp.int32, (N_KEYS, LANES), 0).astype(F32)

        def build(p, carry):
            bits = _gate_tile_bits(p, keys, e1_ref, e2_ref, g_ref)
            bits = bits | (_gate_tile_bits(p + half, keys, e1_ref, e2_ref, g_ref) >> 16)
            c_ref[pl.ds(pl.multiple_of(p * C_PITCH, 4), N_KEYS), :] = bits
            return carry

        lax.fori_loop(0, half, build, 0, unroll=32)
        hid_a_ref[...] = lax.dot_general(xn_ref[...], u0_ref[...], _NT, preferred_element_type=F32)

    def weighted(hid_ref, first_key):
        parts = []
        for t in range(ck):
            bits = c_ref[pl.ds(first_key + t, half, stride=C_PITCH), :]
            c = jnp.concatenate([lax.bitcast_convert_type(bits & high, F32),
                                 lax.bitcast_convert_type(bits << 16, F32)], axis=0)
            parts.append((c * jax.nn.gelu(hid_ref[:, t * N_KEYS:(t + 1) * N_KEYS])).astype(BF16))
        return parts

    head = (j // lane_blocks) % PEER_HEADS
    lane_block = j % lane_blocks
    tok0 = pl.multiple_of(lane_block * LANES, LANES)
    xn = xn_ref[...]
    rows = ck * N_KEYS
    routing = _route_unit(qn_ref[2 * head, pl.ds(tok0, LANES), :], qn_ref[2 * head + 1, pl.ds(tok0, LANES), :],
                          k1_ref[...], k2_ref[...])

    def route_some(count):
        for _ in range(count):
            next(routing, None)

    def add_weighted(w, v_rows, extractions):
        for col in range(0, y_ref.shape[1], MXU_COLS):
            y_ref[:, col:col + MXU_COLS] += jnp.dot(w, v_ref[v_rows:v_rows + rows, col:col + MXU_COLS],
                                                    preferred_element_type=F32)
            route_some(extractions)

    w_a = jnp.concatenate(weighted(hid_a_ref, 2 * ck * j), axis=1)
    hid_b_ref[...] = lax.dot_general(xn, ua_ref[...], _NT, preferred_element_type=F32)
    route_some(4)
    add_weighted(w_a, 0, 2)
    w_b = jnp.concatenate(weighted(hid_b_ref, 2 * ck * j + ck), axis=1)
    hid_a_ref[...] = lax.dot_general(xn, ub_ref[...], _NT, preferred_element_type=F32)
    route_some(4)
    add_weighted(w_b, rows, 3)
    e1n, e2n, gn = _finish(routing)

    slot0 = pl.multiple_of(head * PEER_TOPK, PEER_TOPK)
    e1n_ref[lane_block, pl.ds(slot0, PEER_TOPK), :] = e1n
    e2n_ref[lane_block, pl.ds(slot0, PEER_TOPK), :] = e2n
    gn_ref[lane_block, pl.ds(slot0, PEER_TOPK), :] = gn

    @pl.when(j == n_steps - 1)
    def _():
        x2 = x1_ref[...] + y_ref[...]
        ms = jnp.mean(x2 * x2, axis=-1, keepdims=True)
        y_ref[...] = x2 * lax.rsqrt(ms + RMS_EPS) * nf_ref[...]
        for lb in range(lane_blocks):
            e1_ref[lb * LANES:(lb + 1) * LANES, :] = e1n_ref[lb].T
            e2_ref[lb * LANES:(lb + 1) * LANES, :] = e2n_ref[lb].T
            g_ref[lb * LANES:(lb + 1) * LANES, :] = gn_ref[lb].T


def _peer_experts(xn, q3, e1_first, e2_first, g_first, k1, k2, u, v, x1, norm_final):
    n, d = xn.shape
    tn = e1_first.shape[0]
    rows = PEER_CHUNK_KEYS * N_KEYS
    n_chunks = u.shape[0] // rows
    n_blocks = n // tn
    slots = PEER_HEADS * PEER_TOPK
    assert n_chunks % 2 == 0 and n % tn == 0 and tn % LANES == 0
    assert n_chunks // 2 >= PEER_HEADS * (tn // LANES), "one routing unit per grid step must cover the next block"
    blk = lambda i, j: (i, 0)
    fixed = lambda i, j: (0, 0)
    first = pl.BlockSpec((tn, slots), fixed)
    kernel = functools.partial(_peer_kernel, tn=tn)
    return pl.pallas_call(
        kernel,
        grid=(n_blocks, n_chunks // 2),
        in_specs=[pl.BlockSpec((tn, d), blk, pipeline_mode=pl.Buffered(1)),
                  pl.BlockSpec((q3.shape[0], tn, LANES), lambda i, j: (0, jnp.minimum(i + 1, n_blocks - 1), 0)),
                  first, first, first,
                  pl.BlockSpec(k1.shape, fixed), pl.BlockSpec(k2.shape, fixed),
                  pl.BlockSpec((rows, d), fixed, pipeline_mode=pl.Buffered(1)),
                  pl.BlockSpec((rows, d), lambda i, j: (2 * j + 1, 0)),
                  pl.BlockSpec((rows, d), lambda i, j: (jnp.minimum(2 * j + 2, n_chunks - 1), 0)),
                  pl.BlockSpec((2 * rows, d), lambda i, j: (j, 0)),
                  pl.BlockSpec((tn, d), blk, pipeline_mode=pl.Buffered(1)),
                  pl.BlockSpec((1, d), fixed)],
        out_specs=pl.BlockSpec((tn, d), blk),
        out_shape=jax.ShapeDtypeStruct((n, d), F32),
        scratch_shapes=[pltpu.VMEM((tn // 2 * C_PITCH, LANES), jnp.uint32),
                        pltpu.VMEM((tn, rows), F32), pltpu.VMEM((tn, rows), F32)]
                       + [pltpu.VMEM((tn, slots), F32)] * 3
                       + [pltpu.VMEM((tn // LANES, slots, LANES), F32)] * 3,
        compiler_params=_params("arbitrary", "arbitrary"),
        name="peer_experts",
    )(xn, q3, e1_first, e2_first, g_first, k1, k2, u, u, u, v, x1, norm_final.reshape(1, d))


def kernel(x_prompt, x_sample, cache_sb_k, cache_sb_v, state_conv, state_lru, meta_tokens, norm_mix, norm_ffn, w_in, b_gate, conv_w, conv_b, w_rg_a, b_rg_a, w_rg_x, b_rg_x, lru_lambda, w_proj_attn, w_proj_lru, w_out, w_query, sub_keys, expert_u, expert_v, norm_final):
    depth = w_in.shape[0]
    assert depth == 1, "one layer: the meta tokens' residual stream is never read after the mixer state"
    batch, seq, d = x_prompt.shape
    dec_batch, dec_seq, _ = x_sample.shape
    n_cache = cache_sb_k.shape[2]
    heads = cache_sb_k.shape[3]
    n_frames = batch * seq
    n_sample = dec_batch * dec_seq
    n_meta = meta_tokens.shape[0]
    assert n_meta == N_META and d == heads * LANES

    l = 0
    w_in_b = w_in[l].astype(BF16)
    w_a = w_rg_a[l].astype(BF16)
    w_x = w_rg_x[l].astype(BF16)
    w_pa = w_proj_attn[l].astype(BF16)
    w_pl = w_proj_lru[l].astype(BF16)
    w_o = w_out[l].astype(BF16)
    wq_b = w_query[l].astype(BF16)
    k1 = sub_keys[l, 0].astype(BF16)
    k2 = sub_keys[l, 1].astype(BF16)
    u_b = expert_u[l].astype(BF16)
    v_b = expert_v[l].astype(BF16)

    x_f = x_prompt.reshape(n_frames, d)
    x_s = jnp.concatenate([x_sample.reshape(n_sample, d), meta_tokens.astype(x_sample.dtype)], axis=0)

    col_q, col_k, col_v = 0, 1, 2
    other_cols = (3, 4, 5, 6)

    def project(xn, cols, dtype):
        return [_matmul_cols(xn, w_in_b, g, d, dtype) for g in cols]

    xn_f, q_f = _norm_project(x_f, norm_mix[l], w_in_b, col_q, d, BF16)
    xn_s, q_s = _norm_project(x_s, norm_mix[l], w_in_b, col_q, d, BF16)
    xr_f, yg_f, gs_f, gl_f = project(xn_f, other_cols, F32)
    k_s, v_s, xr_s, yg_s, gs_s, gl_s = project(xn_s, (col_k, col_v) + other_cols, F32)
    xn_meta = jnp.broadcast_to(xn_s[n_sample:][None], (batch, n_meta, d))
    xn_p = jnp.concatenate([xn_meta, xn_f.reshape(batch, seq, d)], axis=1).reshape(batch * (n_meta + seq), d)
    k_p, v_p = project(xn_p, (col_k, col_v), F32)

    lru_w = (conv_w[l], conv_b[l], w_a, b_rg_a[l], w_x, b_rg_x[l], lru_lambda[l])
    _, buf_meta, h_meta = _conv_rglru(
        xr_s[n_sample:], yg_s[n_sample:], jnp.zeros((1, CONV_WIDTH - 1, d), F32), jnp.zeros((1, 1, d), F32),
        *lru_w, n_streams=1, n_rows=n_meta, shared_state=True)
    m_f, buf_f, h_f = _conv_rglru(xr_f, yg_f, buf_meta, h_meta, *lru_w,
                                  n_streams=batch, n_rows=seq, shared_state=True)
    m_s, buf_s, h_s = _conv_rglru(xr_s[:n_sample], yg_s[:n_sample], state_conv[l], state_lru[l][:, None, :],
                                  *lru_w, n_streams=dec_batch, n_rows=dec_seq, shared_state=False)

    attn_f = _sb_attention(q_f, k_p, v_p, n_streams=batch, n_new=seq, n_cache=n_meta, heads=heads)
    attn_s = _sb_attention(q_s[:n_sample],
                           (k_s[:n_sample], cache_sb_k[l].reshape(dec_batch * n_cache, d)),
                           (v_s[:n_sample], cache_sb_v[l].reshape(dec_batch * n_cache, d)),
                           n_streams=dec_batch, n_new=dec_seq, n_cache=n_cache, heads=heads)

    def finish(attn, m, gs, gl, x):
        x1, xn2 = _mix_project(attn, m, gs, gl, b_gate[l], x, w_pa, w_pl, w_o, norm_ffn[l])
        q3 = _query_proj(xn2, wq_b)
        tn = _row_block(xn2.shape[0], 512, LANES)
        e1, e2, gate = _peer_route_first(q3, k1, k2, tn)
        return _peer_experts(xn2, q3, e1, e2, gate, k1, k2, u_b, v_b, x1, norm_final)

    y_f = finish(attn_f, m_f, gs_f, gl_f, x_f)
    y_s = finish(attn_s, m_s[:n_sample], gs_s[:n_sample], gl_s[:n_sample], x_s[:n_sample])

    return (y_f.reshape(batch, seq, d), y_s.reshape(dec_batch, dec_seq, d),
            k_p.reshape(1, batch, n_meta + seq, heads, LANES), v_p.reshape(1, batch, n_meta + seq, heads, LANES),
            buf_f[None], h_f.reshape(1, batch, d),
            k_s[:n_sample].reshape(1, dec_batch, dec_seq, heads, LANES),
            v_s[:n_sample].reshape(1, dec_batch, dec_seq, heads, LANES),
            buf_s[None], h_s.reshape(1, dec_batch, d))
```
